```python
import jax, jax.numpy as jnp
from jax import lax
import numpy as np

D_MODEL = 1024
BATCH = 16
SEQ = 4096
DEPTH = 2
DEC_BATCH = 8
DEC_SEQ = 2048
PAST_LEN = 128

N_MIXERS = 2
N_GLA_LAYERS = (DEPTH + 1) // 2
N_MLA_LAYERS = DEPTH // 2
PLE_DIM = 256
D_FF = 2816
NORM_EPS = 1e-6

GLA_HEADS = 4
GLA_DK = D_MODEL // 2 // GLA_HEADS
GLA_DV = D_MODEL // GLA_HEADS
GLA_GATE_RANK = 16
GLA_GATE_TAU = 16.0
GLA_CHUNK = 64
GLA_QK_W = GLA_HEADS * GLA_DK
GLA_V_W = GLA_HEADS * GLA_DV
GLA_IN = 2 * GLA_QK_W + 2 * GLA_V_W + 2 * GLA_GATE_RANK

MLA_HEADS = 8
MLA_Q_RANK = 384
MLA_KV_RANK = 256
MLA_NOPE = 128
MLA_ROPE = 64
MLA_V = 128
MLA_IN = MLA_Q_RANK + MLA_KV_RANK + MLA_ROPE
ROPE_THETA = 10000.0
Q_BLOCK = 128

kernel_name = 'hybrid_gla_mla_macaron_encoder'


def rms_norm(x, g):
    xf = x.astype(jnp.float32)
    y = xf * lax.rsqrt(jnp.mean(xf * xf, axis=-1, keepdims=True) + NORM_EPS)
    return (y * g.astype(jnp.float32)).astype(x.dtype)


def swiglu(h, w_in, w_out):
    gate, up = jnp.split(h @ w_in, 2, axis=-1)
    return (jax.nn.silu(gate) * up) @ w_out


def gla_chunked(q, k, v, g, strict):
    B, H, L, DK = q.shape
    DV = v.shape[-1]
    C = GLA_CHUNK
    N = L // C
    q = q.reshape(B, H, N, C, DK)
    k = k.reshape(B, H, N, C, DK)
    v = v.reshape(B, H, N, C, DV)
    b = jnp.cumsum(g.reshape(B, H, N, C, DK), axis=3)
    b_last = b[:, :, :, -1:, :]
    q_dec = q * jnp.exp(b)
    k_inv = k * jnp.exp(-b)
    k_end = k * jnp.exp(b_last - b)
    mask = jnp.tril(jnp.ones((C, C), dtype=bool), -1 if strict else 0)
    a = jnp.where(mask, jnp.einsum('bhnid,bhnjd->bhnij', q_dec, k_inv), 0.0)
    o_intra = jnp.einsum('bhnij,bhnjv->bhniv', a, v)
    ds = jnp.einsum('bhnjd,bhnjv->bhndv', k_end, v)
    chunk_decay = jnp.exp(b_last[:, :, :, 0, :])

    def step(s, inp):
        q_n, dec_n, ds_n = inp
        o_n = jnp.einsum('bhid,bhdv->bhiv', q_n, s)
        return dec_n[..., None] * s + ds_n, o_n

    s0 = jnp.zeros((B, H, DK, DV), q.dtype)
    _, o_inter = lax.scan(step, s0, (jnp.moveaxis(q_dec, 2, 0), jnp.moveaxis(chunk_decay, 2, 0), jnp.moveaxis(ds, 2, 0)))
    o = o_intra + jnp.moveaxis(o_inter, 0, 2)
    return o.reshape(B, H, L, DV)


def gla_mixer(h, w_in, w_gf_up, b_gf, w_gb_up, b_gb, out_norm, w_out):
    B, S, _ = h.shape
    idx = [GLA_QK_W, 2 * GLA_QK_W, 2 * GLA_QK_W + GLA_V_W, 2 * GLA_QK_W + 2 * GLA_V_W,
           2 * GLA_QK_W + 2 * GLA_V_W + GLA_GATE_RANK]
    q, k, v, r, gf_lo, gb_lo = jnp.split(h @ w_in, idx, axis=-1)

    def heads(t, d):
        return t.reshape(B, S, GLA_HEADS, d).transpose(0, 2, 1, 3).astype(jnp.float32)

    q = heads(q, GLA_DK) * (GLA_DK ** -0.5)
    k = heads(k, GLA_DK)
    v = heads(v, GLA_DV)
    gf = heads(jax.nn.log_sigmoid((gf_lo @ w_gf_up + b_gf).astype(jnp.float32)) / GLA_GATE_TAU, GLA_DK)
    gb = heads(jax.nn.log_sigmoid((gb_lo @ w_gb_up + b_gb).astype(jnp.float32)) / GLA_GATE_TAU, GLA_DK)
    flip = lambda t: jnp.flip(t, axis=2)
    o_f = gla_chunked(q, k, v, gf, False)
    o_b = flip(gla_chunked(flip(q), flip(k), flip(v), flip(gb), True))
    o = (o_f + o_b).transpose(0, 2, 1, 3)
    o = rms_norm(o, out_norm).astype(h.dtype) * jax.nn.silu(r).reshape(B, S, GLA_HEADS, GLA_DV)
    return o.reshape(B, S, GLA_V_W) @ w_out


def rope_tables(S):
    inv_freq = ROPE_THETA ** (-jnp.arange(0, MLA_ROPE, 2, dtype=jnp.float32) / MLA_ROPE)
    ang = jnp.arange(S, dtype=jnp.float32)[:, None] * inv_freq[None, :]
    return jnp.cos(ang), jnp.sin(ang)


def rope(x, cos, sin):
    x1, x2 = jnp.split(x, 2, axis=-1)
    cos = cos.astype(x.dtype)
    sin = sin.astype(x.dtype)
    return jnp.concatenate([x1 * cos - x2 * sin, x2 * cos + x1 * sin], axis=-1)


def mla_mixer(h, w_in, q_norm, kv_norm, w_uq, w_ukv, w_out):
    B, S, _ = h.shape
    cq, ckv, kr = jnp.split(h @ w_in, [MLA_Q_RANK, MLA_Q_RANK + MLA_KV_RANK], axis=-1)
    q = (rms_norm(cq, q_norm) @ w_uq).reshape(B, S, MLA_HEADS, MLA_NOPE + MLA_ROPE)
    q_nope, q_rope = jnp.split(q, [MLA_NOPE], axis=-1)
    kv = (rms_norm(ckv, kv_norm) @ w_ukv).reshape(B, S, MLA_HEADS, MLA_NOPE + MLA_V)
    k_nope, v = jnp.split(kv, [MLA_NOPE], axis=-1)
    cos, sin = rope_tables(S)
    q_rope = rope(q_rope, cos[:, None, :], sin[:, None, :])
    kr = rope(kr, cos, sin)
    scale = (MLA_NOPE + MLA_ROPE) ** -0.5
    nb = S // Q_BLOCK

    def to_blocks(t):
        return jnp.moveaxis(t.reshape(B, nb, Q_BLOCK, *t.shape[2:]), 1, 0)

    def attend(blk):
        qn, qr = blk
        s = jnp.einsum('bqhd,bkhd->bhqk', qn, k_nope) + jnp.einsum('bqhr,bkr->bhqk', qr, kr)
        p = jax.nn.softmax(s.astype(jnp.float32) * scale, axis=-1).astype(v.dtype)
        return jnp.einsum('bhqk,bkhd->bqhd', p, v)

    o = lax.map(attend, (to_blocks(q_nope), to_blocks(q_rope)))
    o = jnp.moveaxis(o, 0, 1).reshape(B, S, MLA_HEADS * MLA_V)
    return o @ w_out


def trunk(x, p, ffn_norm, ffn_w_in, ffn_w_out, mix_norm, ple_norm, ple_w_gate, ple_w_proj,
          gla_w_in, gla_w_gf_up, gla_b_gf, gla_w_gb_up, gla_b_gb, gla_out_norm, gla_w_out,
          mla_w_in, mla_q_norm, mla_kv_norm, mla_w_uq, mla_w_ukv, mla_w_out, final_norm):
    for i in range(DEPTH):
        x = x + 0.5 * swiglu(rms_norm(x, ffn_norm[i, 0]), ffn_w_in[i, 0], ffn_w_out[i, 0])
        h = rms_norm(x, mix_norm[i])
        j = i // N_MIXERS
        if i % N_MIXERS == 0:
            x = x + gla_mixer(h, gla_w_in[j], gla_w_gf_up[j], gla_b_gf[j], gla_w_gb_up[j], gla_b_gb[j],
                              gla_out_norm[j], gla_w_out[j])
        else:
            x = x + mla_mixer(h, mla_w_in[j], mla_q_norm[j], mla_kv_norm[j], mla_w_uq[j], mla_w_ukv[j], mla_w_out[j])
        x = x + 0.5 * swiglu(rms_norm(x, ffn_norm[i, 1]), ffn_w_in[i, 1], ffn_w_out[i, 1])
        gate = jax.nn.sigmoid(rms_norm(x, ple_norm[i]) @ ple_w_gate[i])
        x = x + gate * (p[i] @ ple_w_proj[i])
    return rms_norm(x, final_norm)


def _dense(key, shape, fan_in):
    return jax.random.normal(key, shape, jnp.float32) * (fan_in ** -0.5)


def _gain(key, shape):
    return 1.0 + 0.02 * jax.random.normal(key, shape, jnp.float32)


def setup_inputs(seed: int = 0) -> dict:
    key = jax.random.key(seed)
    ks = jax.random.split(key, 26)
    NG, NM = N_GLA_LAYERS, N_MLA_LAYERS
    return {
        'x_prompt': jax.random.normal(ks[0], (BATCH, SEQ, D_MODEL), jnp.float32),
        'x_sample': jax.random.normal(ks[1], (DEC_BATCH, DEC_SEQ, D_MODEL), jnp.float32),
        'p_prompt': jax.random.normal(ks[2], (DEPTH, BATCH, SEQ, PLE_DIM), jnp.float32),
        'p_sample': jax.random.normal(ks[3], (DEPTH, DEC_BATCH, DEC_SEQ, PLE_DIM), jnp.float32),
        'ffn_norm': _gain(ks[4], (DEPTH, 2, D_MODEL)),
        'ffn_w_in': _dense(ks[5], (DEPTH, 2, D_MODEL, 2 * D_FF), D_MODEL),
        'ffn_w_out': _dense(ks[6], (DEPTH, 2, D_FF, D_MODEL), D_FF),
        'mix_norm': _gain(ks[7], (DEPTH, D_MODEL)),
        'ple_norm': _gain(ks[8], (DEPTH, D_MODEL)),
        'ple_w_gate': _dense(ks[9], (DEPTH, D_MODEL, D_MODEL), D_MODEL),
        'ple_w_proj': _dense(ks[10], (DEPTH, PLE_DIM, D_MODEL), PLE_DIM),
        'gla_w_in': _dense(ks[11], (NG, D_MODEL, GLA_IN), D_MODEL),
        'gla_w_gf_up': _dense(ks[12], (NG, GLA_GATE_RANK, GLA_QK_W), GLA_GATE_RANK),
        'gla_b_gf': 0.1 * jax.random.normal(ks[13], (NG, GLA_QK_W), jnp.float32),
        'gla_w_gb_up': _dense(ks[14], (NG, GLA_GATE_RANK, GLA_QK_W), GLA_GATE_RANK),
        'gla_b_gb': 0.1 * jax.random.normal(ks[15], (NG, GLA_QK_W), jnp.float32),
        'gla_out_norm': _gain(ks[16], (NG, GLA_DV)),
        'gla_w_out': _dense(ks[17], (NG, GLA_V_W, D_MODEL), GLA_V_W),
        'mla_w_in': _dense(ks[18], (NM, D_MODEL, MLA_IN), D_MODEL),
        'mla_q_norm': _gain(ks[19], (NM, MLA_Q_RANK)),
        'mla_kv_norm': _gain(ks[20], (NM, MLA_KV_RANK)),
        'mla_w_uq': _dense(ks[21], (NM, MLA_Q_RANK, MLA_HEADS * (MLA_NOPE + MLA_ROPE)), MLA_Q_RANK),
        'mla_w_ukv': _dense(ks[22], (NM, MLA_KV_RANK, MLA_HEADS * (MLA_NOPE + MLA_V)), MLA_KV_RANK),
        'mla_w_out': _dense(ks[23], (NM, MLA_HEADS * MLA_V, D_MODEL), MLA_HEADS * MLA_V),
        'final_norm': _gain(ks[24], (D_MODEL,)),
    }


def reference(x_prompt, x_sample, p_prompt, p_sample, ffn_norm, ffn_w_in, ffn_w_out, mix_norm, ple_norm,
              ple_w_gate, ple_w_proj, gla_w_in, gla_w_gf_up, gla_b_gf, gla_w_gb_up, gla_b_gb, gla_out_norm,
              gla_w_out, mla_w_in, mla_q_norm, mla_kv_norm, mla_w_uq, mla_w_ukv, mla_w_out, final_norm):
    weights = (ffn_norm, ffn_w_in, ffn_w_out, mix_norm, ple_norm, ple_w_gate, ple_w_proj,
               gla_w_in, gla_w_gf_up, gla_b_gf, gla_w_gb_up, gla_b_gb, gla_out_norm, gla_w_out,
               mla_w_in, mla_q_norm, mla_kv_norm, mla_w_uq, mla_w_ukv, mla_w_out, final_norm)
    y_prompt = trunk(x_prompt, p_prompt, *weights)
    y_sample = trunk(x_sample, p_sample, *weights)
    return (y_prompt, y_sample)
```

```python
import functools

import jax
import jax.numpy as jnp
from jax import lax
from jax.experimental import pallas as pl
from jax.experimental.pallas import tpu as pltpu

F32 = jnp.float32
BF16 = jnp.bfloat16

D_MODEL = 1024
DEPTH = 2
PLE_DIM = 256
D_FF = 2816
NORM_EPS = 1e-6

GLA_HEADS = 4
GLA_DK = 128
GLA_DV = 256
GLA_GATE_RANK = 16
GLA_GATE_TAU = 16.0
GLA_CHUNK = 64
GLA_QK_W = GLA_HEADS * GLA_DK
GLA_V_W = GLA_HEADS * GLA_DV

MLA_HEADS = 8
MLA_Q_RANK = 384
MLA_KV_RANK = 256
MLA_NOPE = 128
MLA_ROPE = 64
MLA_V = 128
ROPE_THETA = 10000.0

LANES = 128
MLA_QK_PAD = 2 * LANES
VMEM_LIMIT_BYTES = 56 * 1024 * 1024

FFN_CHUNK = 256
GLA_SUB = 256


def _params(*sem):
    return pltpu.CompilerParams(dimension_semantics=sem, vmem_limit_bytes=VMEM_LIMIT_BYTES)


def _const_spec(shape):
    nd = len(shape)
    return pl.BlockSpec(shape, lambda *_: (0,) * nd)


def _row_tile(n, pref):
    t = min(pref, n)
    assert n % t == 0, (n, t)
    return t


def _rms(x, g):
    ms = jnp.mean(x * x, axis=-1, keepdims=True)
    return x * lax.rsqrt(ms + NORM_EPS) * g


def _dot(a, b):
    return jnp.dot(a, b, preferred_element_type=F32)


def _dot_nt(a, b):
    return lax.dot_general(a, b, (((1,), (1,)), ((), ())), preferred_element_type=F32)


def _dot_tn(a, b):
    return lax.dot_general(a, b, (((0,), (0,)), ((), ())), preferred_element_type=F32)


def _ffn_kernel(x_ref, g_ref, win_ref, wout_ref, o_ref, acc_ref, *, nck, ck):
    x = x_ref[...]
    xn = _rms(x, g_ref[...]).astype(BF16)

    def body(j, carry):
        h = _dot(xn, win_ref[j])
        gate = h[:, :ck]
        up = h[:, ck:]
        a = (gate * jax.nn.sigmoid(gate) * up).astype(BF16)
        acc_ref[...] += _dot(a, wout_ref[j])
        return carry

    acc_ref[...] = jnp.zeros_like(acc_ref)
    lax.fori_loop(0, nck, body, 0)
    o_ref[...] = x + 0.5 * acc_ref[...]


def _ffn(x, g, w_in3, w_out3, *, tm=512):
    T, D = x.shape
    nck, _, ck2 = w_in3.shape
    ck = ck2 // 2
    tm = _row_tile(T, tm)
    return pl.pallas_call(
        functools.partial(_ffn_kernel, nck=nck, ck=ck),
        grid=(T // tm,),
        in_specs=[
            pl.BlockSpec((tm, D), lambda i: (i, 0)),
            _const_spec((1, D)),
            _const_spec(w_in3.shape),
            _const_spec(w_out3.shape),
        ],
        out_specs=pl.BlockSpec((tm, D), lambda i: (i, 0)),
        out_shape=jax.ShapeDtypeStruct((T, D), F32),
        scratch_shapes=[pltpu.VMEM((tm, D), F32)],
        compiler_params=_params("arbitrary"),
        name="ffn",
    )(x, g, w_in3, w_out3)


def _proj_res_kernel(x_ref, a_ref, w_ref, o_ref):
    o_ref[...] = x_ref[...] + _dot(a_ref[...], w_ref[...])


def _proj_res(x, a, w, *, tm=1024):
    T, D = x.shape
    K = a.shape[1]
    tm = _row_tile(T, tm)
    return pl.pallas_call(
        _proj_res_kernel,
        grid=(T // tm,),
        in_specs=[
            pl.BlockSpec((tm, D), lambda i: (i, 0)),
            pl.BlockSpec((tm, K), lambda i: (i, 0)),
            _const_spec(w.shape),
        ],
        out_specs=pl.BlockSpec((tm, D), lambda i: (i, 0)),
        out_shape=jax.ShapeDtypeStruct((T, D), F32),
        compiler_params=_params("arbitrary"),
        name="proj_res",
    )(x, a, w)


def _ple_kernel(x_ref, p_ref, g_ref, wg_ref, wp_ref, gf_ref, o_ref, *, final):
    x = x_ref[...]
    gate = jax.nn.sigmoid(_dot(_rms(x, g_ref[...]).astype(BF16), wg_ref[...]))
    y = x + gate * _dot(p_ref[...].astype(BF16), wp_ref[...])
    if final:
        y = _rms(y, gf_ref[...])
    o_ref[...] = y


def _ple(x, p, g, wg, wp, g_final, *, final, tm=1024):
    T, D = x.shape
    P = p.shape[1]
    tm = _row_tile(T, tm)
    return pl.pallas_call(
        functools.partial(_ple_kernel, final=final),
        grid=(T // tm,),
        in_specs=[
            pl.BlockSpec((tm, D), lambda i: (i, 0)),
            pl.BlockSpec((tm, P), lambda i: (i, 0)),
            _const_spec((1, D)),
            _const_spec(wg.shape),
            _const_spec(wp.shape),
            _const_spec((1, D)),
        ],
        out_specs=pl.BlockSpec((tm, D), lambda i: (i, 0)),
        out_shape=jax.ShapeDtypeStruct((T, D), F32),
        compiler_params=_params("arbitrary"),
        name="ple",
    )(x, p, g, wg, wp, g_final)


def _log_sigmoid(x):
    return jnp.minimum(x, 0.0) - jnp.log1p(jnp.exp(-jnp.abs(x)))


def _gla_proj_kernel(x_ref, g_ref, w_ref, wlo_ref, wup_ref, bup_ref,
                     q_ref, k_ref, v_ref, r_ref, gf_ref, gb_ref):
    xn = _rms(x_ref[...], g_ref[...]).astype(BF16)
    h = _dot(xn, w_ref[...])
    q_ref[...] = h[:, :GLA_QK_W] * (GLA_DK ** -0.5)
    k_ref[...] = h[:, GLA_QK_W:2 * GLA_QK_W]
    v_ref[...] = h[:, 2 * GLA_QK_W:2 * GLA_QK_W + GLA_V_W].astype(BF16)
    r_ref[...] = h[:, 2 * GLA_QK_W + GLA_V_W:]
    lo = _dot(xn, wlo_ref[...]).astype(BF16)
    pre = _dot(lo, wup_ref[...]) + bup_ref[...]
    gates = _log_sigmoid(pre) / GLA_GATE_TAU
    gf_ref[...] = gates[:, :GLA_QK_W]
    gb_ref[...] = gates[:, GLA_QK_W:]


def _gla_proj(x, g, w, wlo, wup, bup, *, tm=512):
    T, D = x.shape
    tm = _row_tile(T, tm)
    row = lambda n: pl.BlockSpec((tm, n), lambda i: (i, 0))
    return pl.pallas_call(
        _gla_proj_kernel,
        grid=(T // tm,),
        in_specs=[row(D), _const_spec((1, D)), _const_spec(w.shape), _const_spec(wlo.shape),
                  _const_spec(wup.shape), _const_spec(bup.shape)],
        out_specs=[row(GLA_QK_W), row(GLA_QK_W), row(GLA_V_W), row(GLA_V_W), row(GLA_QK_W), row(GLA_QK_W)],
        out_shape=[
            jax.ShapeDtypeStruct((T, GLA_QK_W), F32),
            jax.ShapeDtypeStruct((T, GLA_QK_W), F32),
            jax.ShapeDtypeStruct((T, GLA_V_W), BF16),
            jax.ShapeDtypeStruct((T, GLA_V_W), F32),
            jax.ShapeDtypeStruct((T, GLA_QK_W), F32),
            jax.ShapeDtypeStruct((T, GLA_QK_W), F32),
        ],
        compiler_params=_params("arbitrary"),
        name="gla_proj",
    )(x, g, w, wlo, wup, bup)


def _split3(g):
    hi = g.astype(BF16)
    r1 = g - hi.astype(F32)
    mid = r1.astype(BF16)
    lo = (r1 - mid.astype(F32)).astype(BF16)
    return hi, mid, lo


def _gla_core_kernel(q_ref, k_ref, v_ref, gf_ref, gb_ref, r_ref, gn_ref, o_ref,
                     oacc_ref, sf_ref, sb_ref, *, L, T):
    C = GLA_CHUNK
    nc = T // C
    nsub = L // T
    DK = GLA_DK

    row = lax.broadcasted_iota(jnp.int32, (T, T), 0)
    col = lax.broadcasted_iota(jnp.int32, (T, T), 1)
    cbits = C.bit_length() - 1
    same = lax.shift_right_logical(row, cbits) == lax.shift_right_logical(col, cbits)
    mask_f = same & (col <= row)
    mask_b = same & (col > row)
    tri = jnp.where(mask_f, 1.0, 0.0).astype(BF16)

    def chunk_last(x):
        n = x.shape[1]
        x3 = x.reshape(nc, C, n)
        return jnp.broadcast_to(x3[:, C - 1:C, :], (nc, C, n)).reshape(T, n)

    def prefix(g):
        hi, mid, lo = _split3(g)
        return _dot(tri, hi) + _dot(tri, mid) + _dot(tri, lo)

    def decay_cols(tot_row):
        return jnp.transpose(jnp.broadcast_to(jnp.exp(tot_row), (DK, DK)))

    def apply_decay(s, dec):
        return jnp.concatenate([s[:, :DK] * dec, s[:, DK:] * dec], axis=1)

    def load(t):
        rows = pl.ds(pl.multiple_of(t * T, T), T)
        return (rows, q_ref[0, rows, :], k_ref[0, rows, :], v_ref[0, rows, :],
                gf_ref[0, rows, :], gb_ref[0, rows, :])

    def bwd_terms(gb, k, q):
        pb = prefix(gb)
        totb = chunk_last(pb)
        rb = totb - pb + gb
        return totb, rb

    sf_ref[...] = jnp.zeros_like(sf_ref)
    sb_ref[...] = jnp.zeros_like(sb_ref)

    def fwd_body(t, carry):
        rows, q, k, v, gf, gb = load(t)
        bf = prefix(gf)
        totf = chunk_last(bf)
        qdf = (q * jnp.exp(bf)).astype(BF16)
        kif = (k * jnp.exp(-bf)).astype(BF16)
        kef = (k * jnp.exp(totf - bf)).astype(BF16)
        totb, rb = bwd_terms(gb, k, q)
        qdb = (q * jnp.exp(rb)).astype(BF16)
        kib = (k * jnp.exp(-rb)).astype(BF16)
        a = jnp.where(mask_f, _dot_nt(qdf, kif), jnp.where(mask_b, _dot_nt(qdb, kib), 0.0))
        o = _dot(a.astype(BF16), v)
        outs = []
        for c in range(nc):
            sl = slice(c * C, (c + 1) * C)
            s = sf_ref[...]
            outs.append(o[sl] + _dot(qdf[sl], s.astype(BF16)))
            dec = decay_cols(totf[c * C:c * C + 1, :])
            sf_ref[...] = apply_decay(s, dec) + _dot_tn(kef[sl], v[sl])
        oacc_ref[rows, :] = jnp.concatenate(outs, axis=0)
        return carry

    lax.fori_loop(0, nsub, fwd_body, 0)

    gn = gn_ref[...]

    def bwd_body(i, carry):
        t = nsub - 1 - i
        rows, q, k, v, gf, gb = load(t)
        totb, rb = bwd_terms(gb, k, q)
        qdb = (q * jnp.exp(rb)).astype(BF16)
        keb = (k * jnp.exp(totb - rb)).astype(BF16)
        outs = [None] * nc
        for c in reversed(range(nc)):
            sl = slice(c * C, (c + 1) * C)
            s = sb_ref[...]
            outs[c] = _dot(qdb[sl], s.astype(BF16))
            dec = decay_cols(totb[c * C:c * C + 1, :])
            sb_ref[...] = apply_decay(s, dec) + _dot_tn(keb[sl], v[sl])
        o = oacc_ref[rows, :] + jnp.concatenate(outs, axis=0)
        r = r_ref[0, rows, :]
        o_ref[0, rows, :] = (_rms(o, gn) * (r * jax.nn.sigmoid(r))).astype(BF16)
        return carry

    lax.fori_loop(0, nsub, bwd_body, 0)


def _gla_core(q, k, v, gf, gb, r, gn):
    B, L, _ = q.shape
    T = min(GLA_SUB, L)
    assert L % T == 0 and T % GLA_CHUNK == 0
    qk = pl.BlockSpec((1, L, GLA_DK), lambda b, h: (b, 0, h))
    vv = pl.BlockSpec((1, L, GLA_DV), lambda b, h: (b, 0, h))
    return pl.pallas_call(
        functools.partial(_gla_core_kernel, L=L, T=T),
        grid=(B, GLA_HEADS),
        in_specs=[qk, qk, vv, qk, qk, vv, _const_spec((1, GLA_DV))],
        out_specs=vv,
        out_shape=jax.ShapeDtypeStruct((B, L, GLA_V_W), BF16),
        scratch_shapes=[pltpu.VMEM((L, GLA_DV), F32),
                        pltpu.VMEM((GLA_DK, GLA_DV), F32),
                        pltpu.VMEM((GLA_DK, GLA_DV), F32)],
        compiler_params=_params("arbitrary", "arbitrary"),
        name="gla_core",
    )(q, k, v, gf, gb, r, gn)


def _mla_proj_kernel(x_ref, g_ref, win_ref, qn_ref, kvn_ref, wuq_ref, wukv_ref, cos_ref, sin_ref,
                     q_ref, k_ref, v_ref):
    H = MLA_HEADS
    xn = _rms(x_ref[...], g_ref[...]).astype(BF16)
    h = _dot(xn, win_ref[...])
    cq = h[:, :MLA_Q_RANK]
    ckv = h[:, MLA_Q_RANK:MLA_Q_RANK + MLA_KV_RANK]
    o = MLA_Q_RANK + MLA_KV_RANK
    cos = cos_ref[...]
    sin = sin_ref[...]
    kr = (h[:, o:o + LANES] * cos + h[:, o + LANES:o + 2 * LANES] * sin).astype(BF16)
    qa = _dot(_rms(cq, qn_ref[...]).astype(BF16), wuq_ref[...])
    kv = _dot(_rms(ckv, kvn_ref[...]).astype(BF16), wukv_ref[...])
    scale = (MLA_NOPE + MLA_ROPE) ** -0.5
    for hd in range(H):
        nope = qa[:, hd * LANES:(hd + 1) * LANES]
        rp = qa[:, (H + hd) * LANES:(H + hd + 1) * LANES]
        rs = qa[:, (2 * H + hd) * LANES:(2 * H + hd + 1) * LANES]
        q_ref[:, hd * MLA_QK_PAD:hd * MLA_QK_PAD + LANES] = (nope * scale).astype(BF16)
        q_ref[:, hd * MLA_QK_PAD + LANES:(hd + 1) * MLA_QK_PAD] = ((rp * cos + rs * sin) * scale).astype(BF16)
        k_ref[:, hd * MLA_QK_PAD:hd * MLA_QK_PAD + LANES] = kv[:, hd * LANES:(hd + 1) * LANES].astype(BF16)
        k_ref[:, hd * MLA_QK_PAD + LANES:(hd + 1) * MLA_QK_PAD] = kr
    v_ref[...] = kv[:, H * LANES:].astype(BF16)


def _mla_proj(x, g, win, qn, kvn, wuq, wukv, cos, sin, *, S, tm=512):
    T, D = x.shape
    tm = _row_tile(S, tm)
    ns = S // tm
    row = lambda n: pl.BlockSpec((tm, n), lambda i: (i, 0))
    pos = pl.BlockSpec((tm, LANES), lambda i: (i % ns, 0))
    QW = MLA_HEADS * MLA_QK_PAD
    VW = MLA_HEADS * MLA_V
    return pl.pallas_call(
        _mla_proj_kernel,
        grid=(T // tm,),
        in_specs=[row(D), _const_spec((1, D)), _const_spec(win.shape), _const_spec(qn.shape),
                  _const_spec(kvn.shape), _const_spec(wuq.shape), _const_spec(wukv.shape), pos, pos],
        out_specs=[row(QW), row(QW), row(VW)],
        out_shape=[jax.ShapeDtypeStruct((T, QW), BF16),
                   jax.ShapeDtypeStruct((T, QW), BF16),
                   jax.ShapeDtypeStruct((T, VW), BF16)],
        compiler_params=_params("arbitrary"),
        name="mla_proj",
    )(x, g, win, qn, kvn, wuq, wukv, cos, sin)


def _mla_attn_kernel(q_ref, k_ref, v_ref, o_ref, *, S, tk):
    q = q_ref[0]
    tq = q.shape[0]

    def body(j, carry):
        m, l, acc = carry
        rows = pl.ds(pl.multiple_of(j * tk, tk), tk)
        s = _dot_nt(q, k_ref[0, rows, :])
        m_new = jnp.maximum(m, jnp.max(s, axis=-1, keepdims=True))
        p = jnp.exp(s - m_new)
        alpha = jnp.exp(m - m_new)
        l = alpha * l + jnp.sum(p, axis=-1, keepdims=True)
        acc = alpha * acc + _dot(p.astype(BF16), v_ref[0, rows, :])
        return m_new, l, acc

    init = (jnp.full((tq, 1), -jnp.inf, F32), jnp.zeros((tq, 1), F32), jnp.zeros((tq, MLA_V), F32))
    _, l, acc = lax.fori_loop(0, S // tk, body, init)
    o_ref[0] = (acc / l).astype(BF16)


def _mla_attn(q, k, v, *, tq=512, tk=512):
    B, S, _ = q.shape
    tq = _row_tile(S, tq)
    tk = _row_tile(S, tk)
    return pl.pallas_call(
        functools.partial(_mla_attn_kernel, S=S, tk=tk),
        grid=(B, MLA_HEADS, S // tq),
        in_specs=[
            pl.BlockSpec((1, tq, MLA_QK_PAD), lambda b, h, i: (b, i, h)),
            pl.BlockSpec((1, S, MLA_QK_PAD), lambda b, h, i: (b, 0, h)),
            pl.BlockSpec((1, S, MLA_V), lambda b, h, i: (b, 0, h)),
        ],
        out_specs=pl.BlockSpec((1, tq, MLA_V), lambda b, h, i: (b, i, h)),
        out_shape=jax.ShapeDtypeStruct((B, S, MLA_HEADS * MLA_V), BF16),
        compiler_params=_params("arbitrary", "arbitrary", "arbitrary"),
        name="mla_attn",
    )(q, k, v)


def _prep_weights(ffn_norm, ffn_w_in, ffn_w_out, mix_norm, ple_norm, ple_w_gate, ple_w_proj,
                  gla_w_in, gla_w_gf_up, gla_b_gf, gla_w_gb_up, gla_b_gb, gla_out_norm, gla_w_out,
                  mla_w_in, mla_q_norm, mla_kv_norm, mla_w_uq, mla_w_ukv, mla_w_out, final_norm):
    nck = D_FF // FFN_CHUNK
    w = {}
    gate = ffn_w_in[..., :D_FF].reshape(DEPTH, 2, D_MODEL, nck, FFN_CHUNK)
    up = ffn_w_in[..., D_FF:].reshape(DEPTH, 2, D_MODEL, nck, FFN_CHUNK)
    w["ffn_in"] = jnp.concatenate([gate, up], axis=-1).transpose(0, 1, 3, 2, 4).astype(BF16)
    w["ffn_out"] = ffn_w_out.reshape(DEPTH, 2, nck, FFN_CHUNK, D_MODEL).astype(BF16)
    w["ffn_norm"] = ffn_norm.reshape(DEPTH, 2, 1, D_MODEL)
    w["mix_norm"] = mix_norm.reshape(DEPTH, 1, D_MODEL)
    w["ple_norm"] = ple_norm.reshape(DEPTH, 1, D_MODEL)
    w["ple_gate"] = ple_w_gate.astype(BF16)
    w["ple_proj"] = ple_w_proj.astype(BF16)
    w["final_norm"] = final_norm.reshape(1, D_MODEL)

    main = 2 * GLA_QK_W + 2 * GLA_V_W
    NG = gla_w_in.shape[0]
    R = GLA_GATE_RANK
    w["gla_in"] = gla_w_in[..., :main].astype(BF16)
    w["gla_lo"] = jnp.pad(gla_w_in[..., main:], ((0, 0), (0, 0), (0, LANES - 2 * R))).astype(BF16)
    upm = jnp.zeros((NG, LANES, 2 * GLA_QK_W), F32)
    upm = upm.at[:, :R, :GLA_QK_W].set(gla_w_gf_up).at[:, R:2 * R, GLA_QK_W:].set(gla_w_gb_up)
    w["gla_up"] = upm.astype(BF16)
    w["gla_bup"] = jnp.concatenate([gla_b_gf, gla_b_gb], axis=-1).reshape(NG, 1, 2 * GLA_QK_W)
    w["gla_out_norm"] = gla_out_norm.reshape(NG, 1, GLA_DV)
    w["gla_out"] = gla_w_out.astype(BF16)

    NM = mla_w_in.shape[0]
    H = MLA_HEADS
    half = MLA_ROPE // 2
    padr = lambda t: jnp.pad(t, [(0, 0)] * (t.ndim - 1) + [(0, LANES - MLA_ROPE)])
    swap = lambda t: jnp.concatenate([t[..., half:], t[..., :half]], axis=-1)
    o = MLA_Q_RANK + MLA_KV_RANK
    kr = mla_w_in[..., o:]
    w["mla_in"] = jnp.concatenate([mla_w_in[..., :o], padr(kr), padr(swap(kr))], axis=-1).astype(BF16)
    uq = mla_w_uq.reshape(NM, MLA_Q_RANK, H, MLA_NOPE + MLA_ROPE)
    nope = uq[..., :MLA_NOPE].reshape(NM, MLA_Q_RANK, H * MLA_NOPE)
    rp = uq[..., MLA_NOPE:]
    w["mla_uq"] = jnp.concatenate(
        [nope, padr(rp).reshape(NM, MLA_Q_RANK, H * LANES), padr(swap(rp)).reshape(NM, MLA_Q_RANK, H * LANES)],
        axis=-1).astype(BF16)
    ukv = mla_w_ukv.reshape(NM, MLA_KV_RANK, H, MLA_NOPE + MLA_V)
    w["mla_ukv"] = jnp.concatenate(
        [ukv[..., :MLA_NOPE].reshape(NM, MLA_KV_RANK, H * MLA_NOPE),
         ukv[..., MLA_NOPE:].reshape(NM, MLA_KV_RANK, H * MLA_V)], axis=-1).astype(BF16)
    w["mla_q_norm"] = mla_q_norm.reshape(NM, 1, MLA_Q_RANK)
    w["mla_kv_norm"] = mla_kv_norm.reshape(NM, 1, MLA_KV_RANK)
    w["mla_out"] = mla_w_out.astype(BF16)
    return w


def _rope_tables(S):
    inv_freq = ROPE_THETA ** (-jnp.arange(0, MLA_ROPE, 2, dtype=F32) / MLA_ROPE)
    ang = jnp.arange(S, dtype=F32)[:, None] * inv_freq[None, :]
    cos, sin = jnp.cos(ang), jnp.sin(ang)
    z = jnp.zeros((S, LANES - MLA_ROPE), F32)
    return jnp.concatenate([cos, cos, z], axis=1), jnp.concatenate([-sin, sin, z], axis=1)


def _trunk(x, p, w):
    B, S, D = x.shape
    T = B * S
    x = x.reshape(T, D)
    p = p.reshape(DEPTH, T, PLE_DIM)
    cos, sin = _rope_tables(S)
    for i in range(DEPTH):
        j = i // 2
        x = _ffn(x, w["ffn_norm"][i, 0], w["ffn_in"][i, 0], w["ffn_out"][i, 0])
        if i % 2 == 0:
            q, k, v, r, gf, gb = _gla_proj(x, w["mix_norm"][i], w["gla_in"][j], w["gla_lo"][j],
                                           w["gla_up"][j], w["gla_bup"][j])
            sh = lambda t: t.reshape(B, S, t.shape[-1])
            o = _gla_core(sh(q), sh(k), sh(v), sh(gf), sh(gb), sh(r), w["gla_out_norm"][j])
            x = _proj_res(x, o.reshape(T, GLA_V_W), w["gla_out"][j])
        else:
            q, k, v = _mla_proj(x, w["mix_norm"][i], w["mla_in"][j], w["mla_q_norm"][j], w["mla_kv_norm"][j],
                                w["mla_uq"][j], w["mla_ukv"][j], cos, sin, S=S)
            sh = lambda t: t.reshape(B, S, t.shape[-1])
            o = _mla_attn(sh(q), sh(k), sh(v))
            x = _proj_res(x, o.reshape(T, MLA_HEADS * MLA_V), w["mla_out"][j])
        x = _ffn(x, w["ffn_norm"][i, 1], w["ffn_in"][i, 1], w["ffn_out"][i, 1])
        x = _ple(x, p[i], w["ple_norm"][i], w["ple_gate"][i], w["ple_proj"][i], w["final_norm"],
                 final=(i == DEPTH - 1))
    return x.reshape(B, S, D)


def kernel(x_prompt, x_sample, p_prompt, p_sample, ffn_norm, ffn_w_in, ffn_w_out, mix_norm, ple_norm,
           ple_w_gate, ple_w_proj, gla_w_in, gla_w_gf_up, gla_b_gf, gla_w_gb_up, gla_b_gb, gla_out_norm,
           gla_w_out, mla_w_in, mla_q_norm, mla_kv_norm, mla_w_uq, mla_w_ukv, mla_w_out, final_norm):
    w = _prep_weights(ffn_norm, ffn_w_in, ffn_w_out, mix_norm, ple_norm, ple_w_gate, ple_w_proj,
                      gla_w_in, gla_w_gf_up, gla_b_gf, gla_w_gb_up, gla_b_gb, gla_out_norm, gla_w_out,
                      mla_w_in, mla_q_norm, mla_kv_norm, mla_w_uq, mla_w_ukv, mla_w_out, final_norm)
    return (_trunk(x_prompt, p_prompt, w), _trunk(x_sample, p_sample, w))
```

```python
import functools

import jax
import jax.numpy as jnp
from jax import lax
from jax.experimental import pallas as pl
from jax.experimental.pallas import tpu as pltpu

F32 = jnp.float32
BF16 = jnp.bfloat16

D_MODEL = 1024
DEPTH = 2
PLE_DIM = 256
D_FF = 2816
NORM_EPS = 1e-6

GLA_HEADS = 4
GLA_DK = 128
GLA_DV = 256
GLA_GATE_RANK = 16
GLA_GATE_TAU = 16.0
GLA_CHUNK = 64
GLA_QK_W = GLA_HEADS * GLA_DK
GLA_V_W = GLA_HEADS * GLA_DV

MLA_HEADS = 8
MLA_Q_RANK = 384
MLA_KV_RANK = 256
MLA_NOPE = 128
MLA_ROPE = 64
MLA_V = 128
ROPE_THETA = 10000.0
LOG2_E = 1.4426950408889634

LANES = 128
MLA_QK_PAD = 2 * LANES
VMEM_LIMIT_BYTES = 56 * 1024 * 1024

FFN_CHUNK = 256
GLA_SUB = 256


def _params(*sem):
    return pltpu.CompilerParams(dimension_semantics=sem, vmem_limit_bytes=VMEM_LIMIT_BYTES)


def _const_spec(shape):
    nd = len(shape)
    return pl.BlockSpec(shape, lambda *_: (0,) * nd)


def _row_tile(n, pref):
    t = min(pref, n)
    assert n % t == 0, (n, t)
    return t


def _rms(x, g):
    ms = jnp.mean(x * x, axis=-1, keepdims=True)
    return x * lax.rsqrt(ms + NORM_EPS) * g


def _dot(a, b):
    return jnp.dot(a, b, preferred_element_type=F32)


def _dot_nt(a, b):
    return lax.dot_general(a, b, (((1,), (1,)), ((), ())), preferred_element_type=F32)


def _dot_tn(a, b):
    return lax.dot_general(a, b, (((0,), (0,)), ((), ())), preferred_element_type=F32)


def _ffn_kernel(x_ref, g_ref, win_ref, wout_ref, o_ref, acc_ref, *, nck, ck):
    x = x_ref[...]
    xn = _rms(x, g_ref[...]).astype(BF16)

    def body(j, carry):
        h = _dot(xn, win_ref[j])
        gate = h[:, :ck]
        up = h[:, ck:]
        a = (gate * jax.nn.sigmoid(gate) * up).astype(BF16)
        acc_ref[...] += _dot(a, wout_ref[j])
        return carry

    acc_ref[...] = jnp.zeros_like(acc_ref)
    lax.fori_loop(0, nck, body, 0, unroll=True)
    o_ref[...] = x + 0.5 * acc_ref[...]


def _ffn(x, g, w_in3, w_out3, *, tm=512):
    T, D = x.shape
    nck, _, ck2 = w_in3.shape
    ck = ck2 // 2
    tm = _row_tile(T, tm)
    return pl.pallas_call(
        functools.partial(_ffn_kernel, nck=nck, ck=ck),
        grid=(T // tm,),
        in_specs=[
            pl.BlockSpec((tm, D), lambda i: (i, 0)),
            _const_spec((1, D)),
            _const_spec(w_in3.shape),
            _const_spec(w_out3.shape),
        ],
        out_specs=pl.BlockSpec((tm, D), lambda i: (i, 0)),
        out_shape=jax.ShapeDtypeStruct((T, D), F32),
        scratch_shapes=[pltpu.VMEM((tm, D), F32)],
        compiler_params=_params("arbitrary"),
        name="ffn",
    )(x, g, w_in3, w_out3)


def _proj_res_kernel(x_ref, a_ref, w_ref, o_ref):
    o_ref[...] = x_ref[...] + _dot(a_ref[...], w_ref[...])


def _proj_res(x, a, w, *, tm=1024):
    T, D = x.shape
    K = a.shape[1]
    tm = _row_tile(T, tm)
    return pl.pallas_call(
        _proj_res_kernel,
        grid=(T // tm,),
        in_specs=[
            pl.BlockSpec((tm, D), lambda i: (i, 0)),
            pl.BlockSpec((tm, K), lambda i: (i, 0)),
            _const_spec(w.shape),
        ],
        out_specs=pl.BlockSpec((tm, D), lambda i: (i, 0)),
        out_shape=jax.ShapeDtypeStruct((T, D), F32),
        compiler_params=_params("arbitrary"),
        name="proj_res",
    )(x, a, w)


def _ple_kernel(x_ref, p_ref, g_ref, wg_ref, wp_ref, gf_ref, o_ref, *, final):
    x = x_ref[...]
    gate = jax.nn.sigmoid(_dot(_rms(x, g_ref[...]).astype(BF16), wg_ref[...]))
    y = x + gate * _dot(p_ref[...].astype(BF16), wp_ref[...])
    if final:
        y = _rms(y, gf_ref[...])
    o_ref[...] = y


def _ple(x, p, g, wg, wp, g_final, *, final, tm=1024):
    T, D = x.shape
    P = p.shape[1]
    tm = _row_tile(T, tm)
    return pl.pallas_call(
        functools.partial(_ple_kernel, final=final),
        grid=(T // tm,),
        in_specs=[
            pl.BlockSpec((tm, D), lambda i: (i, 0)),
            pl.BlockSpec((tm, P), lambda i: (i, 0)),
            _const_spec((1, D)),
            _const_spec(wg.shape),
            _const_spec(wp.shape),
            _const_spec((1, D)),
        ],
        out_specs=pl.BlockSpec((tm, D), lambda i: (i, 0)),
        out_shape=jax.ShapeDtypeStruct((T, D), F32),
        compiler_params=_params("arbitrary"),
        name="ple",
    )(x, p, g, wg, wp, g_final)


def _log_sigmoid(x):
    return jnp.minimum(x, 0.0) - jnp.log1p(jnp.exp(-jnp.abs(x)))


def _gla_proj_kernel(x_ref, g_ref, w_ref, wlo_ref, wup_ref, bup_ref,
                     q_ref, k_ref, v_ref, r_ref, gf_ref, gb_ref):
    xn = _rms(x_ref[...], g_ref[...]).astype(BF16)
    h = _dot(xn, w_ref[...])
    q_ref[...] = h[:, :GLA_QK_W] * (GLA_DK ** -0.5)
    k_ref[...] = h[:, GLA_QK_W:2 * GLA_QK_W]
    v_ref[...] = h[:, 2 * GLA_QK_W:2 * GLA_QK_W + GLA_V_W].astype(BF16)
    r_ref[...] = h[:, 2 * GLA_QK_W + GLA_V_W:]
    lo = _dot(xn, wlo_ref[...]).astype(BF16)
    pre = _dot(lo, wup_ref[...]) + bup_ref[...]
    gates = _log_sigmoid(pre) / GLA_GATE_TAU
    gf_ref[...] = gates[:, :GLA_QK_W]
    gb_ref[...] = gates[:, GLA_QK_W:]


def _gla_proj(x, g, w, wlo, wup, bup, *, tm=512):
    T, D = x.shape
    tm = _row_tile(T, tm)
    row = lambda n: pl.BlockSpec((tm, n), lambda i: (i, 0))
    return pl.pallas_call(
        _gla_proj_kernel,
        grid=(T // tm,),
        in_specs=[row(D), _const_spec((1, D)), _const_spec(w.shape), _const_spec(wlo.shape),
                  _const_spec(wup.shape), _const_spec(bup.shape)],
        out_specs=[row(GLA_QK_W), row(GLA_QK_W), row(GLA_V_W), row(GLA_V_W), row(GLA_QK_W), row(GLA_QK_W)],
        out_shape=[
            jax.ShapeDtypeStruct((T, GLA_QK_W), F32),
            jax.ShapeDtypeStruct((T, GLA_QK_W), F32),
            jax.ShapeDtypeStruct((T, GLA_V_W), BF16),
            jax.ShapeDtypeStruct((T, GLA_V_W), F32),
            jax.ShapeDtypeStruct((T, GLA_QK_W), F32),
            jax.ShapeDtypeStruct((T, GLA_QK_W), F32),
        ],
        compiler_params=_params("arbitrary"),
        name="gla_proj",
    )(x, g, w, wlo, wup, bup)


def _split2(g):
    hi = g.astype(BF16)
    lo = (g - hi.astype(F32)).astype(BF16)
    return hi, lo


def _gla_core_kernel(q_ref, k_ref, v_ref, gf_ref, gb_ref, r_ref, gn_ref, o_ref,
                     oacc_ref, sf_ref, sb_ref, *, L, T):
    C = GLA_CHUNK
    nc = T // C
    nsub = L // T
    DK = GLA_DK

    row = lax.broadcasted_iota(jnp.int32, (T, T), 0)
    col = lax.broadcasted_iota(jnp.int32, (T, T), 1)
    cbits = C.bit_length() - 1
    same = lax.shift_right_logical(row, cbits) == lax.shift_right_logical(col, cbits)
    mask_f = same & (col <= row)
    mask_b = same & (col > row)
    tri = jnp.where(mask_f, 1.0, 0.0).astype(BF16)

    def chunk_last(x):
        n = x.shape[1]
        x3 = x.reshape(nc, C, n)
        return jnp.broadcast_to(x3[:, C - 1:C, :], (nc, C, n)).reshape(T, n)

    def prefix(g):
        hi, lo = _split2(g)
        return _dot(tri, hi) + _dot(tri, lo)

    def decay_cols(tot_row):
        return jnp.transpose(jnp.broadcast_to(jnp.exp(tot_row), (DK, DK)))

    def apply_decay(s, dec):
        return jnp.concatenate([s[:, :DK] * dec, s[:, DK:] * dec], axis=1)

    def block_rows(t):
        return pl.ds(pl.multiple_of(t * T, T), T)

    def direction(rows, lg, tot, mask, s_ref, order):
        q = q_ref[0, rows, :]
        k = k_ref[0, rows, :]
        v = v_ref[0, rows, :]
        qd = (q * jnp.exp(lg)).astype(BF16)
        ki = (k * jnp.exp(-lg)).astype(BF16)
        ke = (k * jnp.exp(tot - lg)).astype(BF16)
        a = jnp.where(mask, _dot_nt(qd, ki), 0.0).astype(BF16)
        o = _dot(a, v)
        outs = [None] * nc
        for c in order:
            sl = slice(c * C, (c + 1) * C)
            s = s_ref[...]
            outs[c] = o[sl] + _dot(qd[sl], s.astype(BF16))
            dec = decay_cols(tot[c * C:c * C + 1, :])
            s_ref[...] = apply_decay(s, dec) + _dot_tn(ke[sl], v[sl])
        return jnp.concatenate(outs, axis=0)

    gn = gn_ref[...]

    def finish(rows, o):
        r = r_ref[0, rows, :]
        o_ref[0, rows, :] = (_rms(o, gn) * (r * jax.nn.sigmoid(r))).astype(BF16)

    def pair(i, first):
        rows_f = block_rows(i)
        rows_b = block_rows(nsub - 1 - i)
        gb = gb_ref[0, rows_b, :]
        p = prefix(jnp.concatenate([gf_ref[0, rows_f, :], gb], axis=1))
        tot = chunk_last(p)
        bf, totf = p[:, :DK], tot[:, :DK]
        pb, totb = p[:, DK:], tot[:, DK:]
        rb = totb - pb + gb
        o_f = direction(rows_f, bf, totf, mask_f, sf_ref, range(nc))
        o_b = direction(rows_b, rb, totb, mask_b, sb_ref, reversed(range(nc)))
        if first:
            oacc_ref[rows_f, :] = o_f
            oacc_ref[rows_b, :] = o_b
        else:
            finish(rows_f, oacc_ref[rows_f, :] + o_f)
            finish(rows_b, oacc_ref[rows_b, :] + o_b)

    sf_ref[...] = jnp.zeros_like(sf_ref)
    sb_ref[...] = jnp.zeros_like(sb_ref)
    half = nsub // 2
    unroll = 4 if half % 4 == 0 else (2 if half % 2 == 0 else 1)
    lax.fori_loop(0, half, lambda i, c: (pair(i, True), c)[1], 0, unroll=unroll)
    lax.fori_loop(half, nsub, lambda i, c: (pair(i, False), c)[1], 0, unroll=unroll)


def _gla_core(q, k, v, gf, gb, r, gn):
    B, L, _ = q.shape
    T = min(GLA_SUB, L)
    assert L % (2 * T) == 0 and T % GLA_CHUNK == 0
    qk = pl.BlockSpec((1, L, GLA_DK), lambda b, h: (b, 0, h))
    vv = pl.BlockSpec((1, L, GLA_DV), lambda b, h: (b, 0, h))
    return pl.pallas_call(
        functools.partial(_gla_core_kernel, L=L, T=T),
        grid=(B, GLA_HEADS),
        in_specs=[qk, qk, vv, qk, qk, vv, _const_spec((1, GLA_DV))],
        out_specs=vv,
        out_shape=jax.ShapeDtypeStruct((B, L, GLA_V_W), BF16),
        scratch_shapes=[pltpu.VMEM((L, GLA_DV), F32),
                        pltpu.VMEM((GLA_DK, GLA_DV), F32),
                        pltpu.VMEM((GLA_DK, GLA_DV), F32)],
        compiler_params=_params("arbitrary", "arbitrary"),
        name="gla_core",
    )(q, k, v, gf, gb, r, gn)


def _mla_proj_kernel(x_ref, g_ref, win_ref, qn_ref, kvn_ref, wuq_ref, wukv_ref, cos_ref, sin_ref,
                     q_ref, k_ref, vt_ref):
    H = MLA_HEADS
    xn = _rms(x_ref[...], g_ref[...]).astype(BF16)
    h = _dot(xn, win_ref[...])
    cq = h[:, :MLA_Q_RANK]
    ckv = h[:, MLA_Q_RANK:MLA_Q_RANK + MLA_KV_RANK]
    o = MLA_Q_RANK + MLA_KV_RANK
    cos = cos_ref[...]
    sin = sin_ref[...]
    kr = (h[:, o:o + LANES] * cos + h[:, o + LANES:o + 2 * LANES] * sin).astype(BF16)
    qa = _dot(_rms(cq, qn_ref[...]).astype(BF16), wuq_ref[...])
    kv = _dot(_rms(ckv, kvn_ref[...]).astype(BF16), wukv_ref[...])
    scale = (MLA_NOPE + MLA_ROPE) ** -0.5 * LOG2_E
    for hd in range(H):
        nope = qa[:, hd * LANES:(hd + 1) * LANES]
        rp = qa[:, (H + hd) * LANES:(H + hd + 1) * LANES]
        rs = qa[:, (2 * H + hd) * LANES:(2 * H + hd + 1) * LANES]
        q_ref[:, hd * MLA_QK_PAD:hd * MLA_QK_PAD + LANES] = (nope * scale).astype(BF16)
        q_ref[:, hd * MLA_QK_PAD + LANES:(hd + 1) * MLA_QK_PAD] = ((rp * cos + rs * sin) * scale).astype(BF16)
        k_ref[:, hd * MLA_QK_PAD:hd * MLA_QK_PAD + LANES] = kv[:, hd * LANES:(hd + 1) * LANES].astype(BF16)
        k_ref[:, hd * MLA_QK_PAD + LANES:(hd + 1) * MLA_QK_PAD] = kr
    vt_ref[0] = jnp.transpose(kv[:, H * LANES:]).astype(BF16)


def _mla_proj(x, g, win, qn, kvn, wuq, wukv, cos, sin, *, S, tm=512):
    T, D = x.shape
    tm = _row_tile(S, tm)
    ns = S // tm
    row = lambda n: pl.BlockSpec((tm, n), lambda i: (i, 0))
    pos = pl.BlockSpec((tm, LANES), lambda i: (i % ns, 0))
    QW = MLA_HEADS * MLA_QK_PAD
    VW = MLA_HEADS * MLA_V
    return pl.pallas_call(
        _mla_proj_kernel,
        grid=(T // tm,),
        in_specs=[row(D), _const_spec((1, D)), _const_spec(win.shape), _const_spec(qn.shape),
                  _const_spec(kvn.shape), _const_spec(wuq.shape), _const_spec(wukv.shape), pos, pos],
        out_specs=[row(QW), row(QW), pl.BlockSpec((1, VW, tm), lambda i: (i // ns, 0, i % ns))],
        out_shape=[jax.ShapeDtypeStruct((T, QW), BF16),
                   jax.ShapeDtypeStruct((T, QW), BF16),
                   jax.ShapeDtypeStruct((T // S, VW, S), BF16)],
        compiler_params=_params("arbitrary"),
        name="mla_proj",
    )(x, g, win, qn, kvn, wuq, wukv, cos, sin)


def _mla_attn_kernel(q_ref, k_ref, vt_ref, o_ref):
    qt = jnp.transpose(q_ref[0].astype(F32)).astype(BF16)
    st = _dot(k_ref[0], qt)
    p = jnp.exp2(st - jnp.max(st, axis=0, keepdims=True))
    l = jnp.sum(p, axis=0, keepdims=True)
    ot = _dot(vt_ref[0], p.astype(BF16))
    o_ref[0] = jnp.transpose(ot / l).astype(BF16)


def _mla_attn(q, k, v, *, tq=512):
    B, S, _ = q.shape
    tq = _row_tile(S, tq)
    return pl.pallas_call(
        _mla_attn_kernel,
        grid=(B, MLA_HEADS, S // tq),
        in_specs=[
            pl.BlockSpec((1, tq, MLA_QK_PAD), lambda b, h, i: (b, i, h)),
            pl.BlockSpec((1, S, MLA_QK_PAD), lambda b, h, i: (b, 0, h)),
            pl.BlockSpec((1, MLA_V, S), lambda b, h, i: (b, h, 0)),
        ],
        out_specs=pl.BlockSpec((1, tq, MLA_V), lambda b, h, i: (b, i, h)),
        out_shape=jax.ShapeDtypeStruct((B, S, MLA_HEADS * MLA_V), BF16),
        compiler_params=_params("arbitrary", "arbitrary", "arbitrary"),
        name="mla_attn",
    )(q, k, v)


def _prep_weights(ffn_norm, ffn_w_in, ffn_w_out, mix_norm, ple_norm, ple_w_gate, ple_w_proj,
                  gla_w_in, gla_w_gf_up, gla_b_gf, gla_w_gb_up, gla_b_gb, gla_out_norm, gla_w_out,
                  mla_w_in, mla_q_norm, mla_kv_norm, mla_w_uq, mla_w_ukv, mla_w_out, final_norm):
    nck = D_FF // FFN_CHUNK
    w = {}
    gate = ffn_w_in[..., :D_FF].reshape(DEPTH, 2, D_MODEL, nck, FFN_CHUNK)
    up = ffn_w_in[..., D_FF:].reshape(DEPTH, 2, D_MODEL, nck, FFN_CHUNK)
    w["ffn_in"] = jnp.concatenate([gate, up], axis=-1).transpose(0, 1, 3, 2, 4).astype(BF16)
    w["ffn_out"] = ffn_w_out.reshape(DEPTH, 2, nck, FFN_CHUNK, D_MODEL).astype(BF16)
    w["ffn_norm"] = ffn_norm.reshape(DEPTH, 2, 1, D_MODEL)
    w["mix_norm"] = mix_norm.reshape(DEPTH, 1, D_MODEL)
    w["ple_norm"] = ple_norm.reshape(DEPTH, 1, D_MODEL)
    w["ple_gate"] = ple_w_gate.astype(BF16)
    w["ple_proj"] = ple_w_proj.astype(BF16)
    w["final_norm"] = final_norm.reshape(1, D_MODEL)

    main = 2 * GLA_QK_W + 2 * GLA_V_W
    NG = gla_w_in.shape[0]
    R = GLA_GATE_RANK
    w["gla_in"] = gla_w_in[..., :main].astype(BF16)
    w["gla_lo"] = jnp.pad(gla_w_in[..., main:], ((0, 0), (0, 0), (0, LANES - 2 * R))).astype(BF16)
    upm = jnp.zeros((NG, LANES, 2 * GLA_QK_W), F32)
    upm = upm.at[:, :R, :GLA_QK_W].set(gla_w_gf_up).at[:, R:2 * R, GLA_QK_W:].set(gla_w_gb_up)
    w["gla_up"] = upm.astype(BF16)
    w["gla_bup"] = jnp.concatenate([gla_b_gf, gla_b_gb], axis=-1).reshape(NG, 1, 2 * GLA_QK_W)
    w["gla_out_norm"] = gla_out_norm.reshape(NG, 1, GLA_DV)
    w["gla_out"] = gla_w_out.astype(BF16)

    NM = mla_w_in.shape[0]
    H = MLA_HEADS
    half = MLA_ROPE // 2
    padr = lambda t: jnp.pad(t, [(0, 0)] * (t.ndim - 1) + [(0, LANES - MLA_ROPE)])
    swap = lambda t: jnp.concatenate([t[..., half:], t[..., :half]], axis=-1)
    o = MLA_Q_RANK + MLA_KV_RANK
    kr = mla_w_in[..., o:]
    w["mla_in"] = jnp.concatenate([mla_w_in[..., :o], padr(kr), padr(swap(kr))], axis=-1).astype(BF16)
    uq = mla_w_uq.reshape(NM, MLA_Q_RANK, H, MLA_NOPE + MLA_ROPE)
    nope = uq[..., :MLA_NOPE].reshape(NM, MLA_Q_RANK, H * MLA_NOPE)
    rp = uq[..., MLA_NOPE:]
    w["mla_uq"] = jnp.concatenate(
        [nope, padr(rp).reshape(NM, MLA_Q_RANK, H * LANES), padr(swap(rp)).reshape(NM, MLA_Q_RANK, H * LANES)],
        axis=-1).astype(BF16)
    ukv = mla_w_ukv.reshape(NM, MLA_KV_RANK, H, MLA_NOPE + MLA_V)
    w["mla_ukv"] = jnp.concatenate(
        [ukv[..., :MLA_NOPE].reshape(NM, MLA_KV_RANK, H * MLA_NOPE),
         ukv[..., MLA_NOPE:].reshape(NM, MLA_KV_RANK, H * MLA_V)], axis=-1).astype(BF16)
    w["mla_q_norm"] = mla_q_norm.reshape(NM, 1, MLA_Q_RANK)
    w["mla_kv_norm"] = mla_kv_norm.reshape(NM, 1, MLA_KV_RANK)
    w["mla_out"] = mla_w_out.astype(BF16)
    return w


def _rope_tables(S):
    inv_freq = ROPE_THETA ** (-jnp.arange(0, MLA_ROPE, 2, dtype=F32) / MLA_ROPE)
    ang = jnp.arange(S, dtype=F32)[:, None] * inv_freq[None, :]
    cos, sin = jnp.cos(ang), jnp.sin(ang)
    z = jnp.zeros((S, LANES - MLA_ROPE), F32)
    return jnp.concatenate([cos, cos, z], axis=1), jnp.concatenate([-sin, sin, z], axis=1)


def _trunk(x, p, w):
    B, S, D = x.shape
    T = B * S
    x = x.reshape(T, D)
    p = p.reshape(DEPTH, T, PLE_DIM)
    cos, sin = _rope_tables(S)
    for i in range(DEPTH):
        j = i // 2
        x = _ffn(x, w["ffn_norm"][i, 0], w["ffn_in"][i, 0], w["ffn_out"][i, 0])
        if i % 2 == 0:
            q, k, v, r, gf, gb = _gla_proj(x, w["mix_norm"][i], w["gla_in"][j], w["gla_lo"][j],
                                           w["gla_up"][j], w["gla_bup"][j])
            sh = lambda t: t.reshape(B, S, t.shape[-1])
            o = _gla_core(sh(q), sh(k), sh(v), sh(gf), sh(gb), sh(r), w["gla_out_norm"][j])
            x = _proj_res(x, o.reshape(T, GLA_V_W), w["gla_out"][j])
        else:
            q, k, vt = _mla_proj(x, w["mix_norm"][i], w["mla_in"][j], w["mla_q_norm"][j], w["mla_kv_norm"][j],
                                 w["mla_uq"][j], w["mla_ukv"][j], cos, sin, S=S)
            sh = lambda t: t.reshape(B, S, t.shape[-1])
            o = _mla_attn(sh(q), sh(k), vt)
            x = _proj_res(x, o.reshape(T, MLA_HEADS * MLA_V), w["mla_out"][j])
        x = _ffn(x, w["ffn_norm"][i, 1], w["ffn_in"][i, 1], w["ffn_out"][i, 1])
        x = _ple(x, p[i], w["ple_norm"][i], w["ple_gate"][i], w["ple_proj"][i], w["final_norm"],
                 final=(i == DEPTH - 1))
    return x.reshape(B, S, D)


def kernel(x_prompt, x_sample, p_prompt, p_sample, ffn_norm, ffn_w_in, ffn_w_out, mix_norm, ple_norm,
           ple_w_gate, ple_w_proj, gla_w_in, gla_w_gf_up, gla_b_gf, gla_w_gb_up, gla_b_gb, gla_out_norm,
           gla_w_out, mla_w_in, mla_q_norm, mla_kv_norm, mla_w_uq, mla_w_ukv, mla_w_out, final_norm):
    w = _prep_weights(ffn_norm, ffn_w_in, ffn_w_out, mix_norm, ple_norm, ple_w_gate, ple_w_proj,
                      gla_w_in, gla_w_gf_up, gla_b_gf, gla_w_gb_up, gla_b_gb, gla_out_norm, gla_w_out,
                      mla_w_in, mla_q_norm, mla_kv_norm, mla_w_uq, mla_w_ukv, mla_w_out, final_norm)
    return (_trunk(x_prompt, p_prompt, w), _trunk(x_sample, p_sample, w))
```

```python
import functools

import jax
import jax.numpy as jnp
from jax import lax
from jax.experimental import pallas as pl
from jax.experimental.pallas import tpu as pltpu

F32 = jnp.float32
BF16 = jnp.bfloat16

D_MODEL = 1024
DEPTH = 2
PLE_DIM = 256
D_FF = 2816
NORM_EPS = 1e-6

GLA_HEADS = 4
GLA_DK = 128
GLA_DV = 256
GLA_GATE_RANK = 16
GLA_GATE_TAU = 16.0
GLA_CHUNK = 64
GLA_QK_W = GLA_HEADS * GLA_DK
GLA_V_W = GLA_HEADS * GLA_DV

MLA_HEADS = 8
MLA_Q_RANK = 384
MLA_KV_RANK = 256
MLA_NOPE = 128
MLA_ROPE = 64
MLA_V = 128
ROPE_THETA = 10000.0
LOG2_E = 1.4426950408889634

LANES = 128
MLA_QK_PAD = 2 * LANES
VMEM_LIMIT_BYTES = 56 * 1024 * 1024

FFN_CHUNK = 256
GLA_SUB = 256
MLA_KEY_CHUNK = 512


def _params(*sem):
    return pltpu.CompilerParams(dimension_semantics=sem, vmem_limit_bytes=VMEM_LIMIT_BYTES)


def _const_spec(shape):
    nd = len(shape)
    return pl.BlockSpec(shape, lambda *_: (0,) * nd)


def _row_tile(n, pref):
    t = min(pref, n)
    assert n % t == 0, (n, t)
    return t


def _rms(x, g):
    ms = jnp.mean(x * x, axis=-1, keepdims=True)
    return x * lax.rsqrt(ms + NORM_EPS) * g


def _dot(a, b):
    return jnp.dot(a, b, preferred_element_type=F32)


def _dot_nt(a, b):
    return lax.dot_general(a, b, (((1,), (1,)), ((), ())), preferred_element_type=F32)


def _dot_tn(a, b):
    return lax.dot_general(a, b, (((0,), (0,)), ((), ())), preferred_element_type=F32)


def _ffn_half_step(x, g_ref, win_ref, wout_ref, acc_ref):
    nck, _, ck2 = win_ref.shape
    ck = ck2 // 2
    xn = _rms(x, g_ref[...]).astype(BF16)

    def body(j, carry):
        h = _dot(xn, win_ref[j])
        gate = h[:, :ck]
        up = h[:, ck:]
        a = (gate * jax.nn.sigmoid(gate) * up).astype(BF16)
        acc_ref[...] += _dot(a, wout_ref[j])
        return carry

    acc_ref[...] = jnp.zeros_like(acc_ref)
    lax.fori_loop(0, nck, body, 0, unroll=True)
    return x + 0.5 * acc_ref[...]


def _ffn_kernel(x_ref, g_ref, win_ref, wout_ref, o_ref, acc_ref):
    o_ref[...] = _ffn_half_step(x_ref[...], g_ref, win_ref, wout_ref, acc_ref)


def _post_mixer_kernel(x_ref, a_ref, wo_ref, g_ref, win_ref, wout_ref, p_ref, gp_ref, wg_ref, wp_ref, gf_ref,
                       o_ref, acc_ref, *, final):
    x = x_ref[...] + _dot(a_ref[...], wo_ref[...])
    x = _ffn_half_step(x, g_ref, win_ref, wout_ref, acc_ref)
    gate = jax.nn.sigmoid(_dot(_rms(x, gp_ref[...]).astype(BF16), wg_ref[...]))
    x = x + gate * _dot(p_ref[...].astype(BF16), wp_ref[...])
    if final:
        x = _rms(x, gf_ref[...])
    o_ref[...] = x


def _layer_spec(arr, *idx):
    n = len(idx)
    rest = arr.shape[n:]
    return pl.BlockSpec((None,) * n + rest, lambda i: idx + (0,) * len(rest), pipeline_mode=pl.Buffered(1))


def _ffn(x, w, li, wi, *, tm=512):
    T, D = x.shape
    tm = _row_tile(T, tm)
    tok = lambda n: pl.BlockSpec((tm, n), lambda i: (i, 0))
    return pl.pallas_call(
        _ffn_kernel,
        grid=(T // tm,),
        in_specs=[tok(D), _layer_spec(w["ffn_norm"], li, wi), _layer_spec(w["ffn_in"], li, wi),
                  _layer_spec(w["ffn_out"], li, wi)],
        out_specs=tok(D),
        out_shape=jax.ShapeDtypeStruct((T, D), F32),
        scratch_shapes=[pltpu.VMEM((tm, D), F32)],
        compiler_params=_params("arbitrary"),
        name="ffn",
    )(x, w["ffn_norm"], w["ffn_in"], w["ffn_out"])


def _post_mixer(x, a, w_o, p, w, li, mi, *, tm=512):
    T, D = x.shape
    tm = _row_tile(T, tm)
    tok = lambda n: pl.BlockSpec((tm, n), lambda i: (i, 0))
    return pl.pallas_call(
        functools.partial(_post_mixer_kernel, final=(li == DEPTH - 1)),
        grid=(T // tm,),
        in_specs=[tok(D), tok(a.shape[1]), _layer_spec(w_o, mi),
                  _layer_spec(w["ffn_norm"], li, 1), _layer_spec(w["ffn_in"], li, 1), _layer_spec(w["ffn_out"], li, 1),
                  pl.BlockSpec((None, tm, PLE_DIM), lambda i: (li, i, 0)),
                  _layer_spec(w["ple_norm"], li), _layer_spec(w["ple_gate"], li), _layer_spec(w["ple_proj"], li),
                  _layer_spec(w["final_norm"])],
        out_specs=tok(D),
        out_shape=jax.ShapeDtypeStruct((T, D), F32),
        scratch_shapes=[pltpu.VMEM((tm, D), F32)],
        compiler_params=_params("arbitrary"),
        name="post_mixer",
    )(x, a, w_o, w["ffn_norm"], w["ffn_in"], w["ffn_out"], p, w["ple_norm"], w["ple_gate"], w["ple_proj"],
      w["final_norm"])


def _log_sigmoid(x):
    return jnp.minimum(x, 0.0) - jnp.log1p(jnp.exp(-jnp.abs(x)))


def _gla_proj_kernel(x_ref, g_ref, w_ref, wlo_ref, wup_ref, bup_ref,
                     q_ref, k_ref, v_ref, r_ref, gf_ref, gb_ref):
    xn = _rms(x_ref[...], g_ref[...]).astype(BF16)
    h = _dot(xn, w_ref[...])
    q_ref[...] = h[:, :GLA_QK_W] * (GLA_DK ** -0.5)
    k_ref[...] = h[:, GLA_QK_W:2 * GLA_QK_W]
    v_ref[...] = h[:, 2 * GLA_QK_W:2 * GLA_QK_W + GLA_V_W].astype(BF16)
    r_ref[...] = h[:, 2 * GLA_QK_W + GLA_V_W:]
    lo = _dot(xn, wlo_ref[...]).astype(BF16)
    pre = _dot(lo, wup_ref[...]) + bup_ref[...]
    gates = _log_sigmoid(pre) / GLA_GATE_TAU
    gf_ref[...] = gates[:, :GLA_QK_W]
    gb_ref[...] = gates[:, GLA_QK_W:]


def _gla_proj(x, g, w, wlo, wup, bup, *, tm=512):
    T, D = x.shape
    tm = _row_tile(T, tm)
    row = lambda n: pl.BlockSpec((tm, n), lambda i: (i, 0))
    return pl.pallas_call(
        _gla_proj_kernel,
        grid=(T // tm,),
        in_specs=[row(D), _const_spec((1, D)), _const_spec(w.shape), _const_spec(wlo.shape),
                  _const_spec(wup.shape), _const_spec(bup.shape)],
        out_specs=[row(GLA_QK_W), row(GLA_QK_W), row(GLA_V_W), row(GLA_V_W), row(GLA_QK_W), row(GLA_QK_W)],
        out_shape=[
            jax.ShapeDtypeStruct((T, GLA_QK_W), F32),
            jax.ShapeDtypeStruct((T, GLA_QK_W), F32),
            jax.ShapeDtypeStruct((T, GLA_V_W), BF16),
            jax.ShapeDtypeStruct((T, GLA_V_W), F32),
            jax.ShapeDtypeStruct((T, GLA_QK_W), F32),
            jax.ShapeDtypeStruct((T, GLA_QK_W), F32),
        ],
        compiler_params=_params("arbitrary"),
        name="gla_proj",
    )(x, g, w, wlo, wup, bup)


def _split2(g):
    hi = g.astype(BF16)
    lo = (g - hi.astype(F32)).astype(BF16)
    return hi, lo


def _gla_core_kernel(q_ref, k_ref, v_ref, gf_ref, gb_ref, r_ref, gn_ref, o_ref,
                     oacc_ref, sf_ref, sb_ref, *, L, T):
    C = GLA_CHUNK
    nc = T // C
    nsub = L // T
    DK = GLA_DK

    row = lax.broadcasted_iota(jnp.int32, (T, T), 0)
    col = lax.broadcasted_iota(jnp.int32, (T, T), 1)
    cbits = C.bit_length() - 1
    same = lax.shift_right_logical(row, cbits) == lax.shift_right_logical(col, cbits)
    mask_f = same & (col <= row)
    mask_b = same & (col > row)
    tri = jnp.where(mask_f, 1.0, 0.0).astype(BF16)

    def chunk_last(x):
        n = x.shape[1]
        x3 = x.reshape(nc, C, n)
        return jnp.broadcast_to(x3[:, C - 1:C, :], (nc, C, n)).reshape(T, n)

    def prefix(g):
        hi, lo = _split2(g)
        return _dot(tri, hi) + _dot(tri, lo)

    def decay_cols(tot_row):
        return jnp.transpose(jnp.broadcast_to(jnp.exp(tot_row), (DK, DK)))

    def apply_decay(s, dec):
        return jnp.concatenate([s[:, :DK] * dec, s[:, DK:] * dec], axis=1)

    def block_rows(t):
        return pl.ds(pl.multiple_of(t * T, T), T)

    def direction(rows, lg, tot, mask, s_ref, order):
        q = q_ref[0, rows, :]
        k = k_ref[0, rows, :]
        v = v_ref[0, rows, :]
        qd = (q * jnp.exp(lg)).astype(BF16)
        ki = (k * jnp.exp(-lg)).astype(BF16)
        ke = (k * jnp.exp(tot - lg)).astype(BF16)
        a = jnp.where(mask, _dot_nt(qd, ki), 0.0).astype(BF16)
        o = _dot(a, v)
        outs = [None] * nc
        for c in order:
            sl = slice(c * C, (c + 1) * C)
            s = s_ref[...]
            outs[c] = o[sl] + _dot(qd[sl], s.astype(BF16))
            dec = decay_cols(tot[c * C:c * C + 1, :])
            s_ref[...] = apply_decay(s, dec) + _dot_tn(ke[sl], v[sl])
        return jnp.concatenate(outs, axis=0)

    gn = gn_ref[...]

    def finish(rows, o):
        r = r_ref[0, rows, :]
        o_ref[0, rows, :] = (_rms(o, gn) * (r * jax.nn.sigmoid(r))).astype(BF16)

    def pair(i, first):
        rows_f = block_rows(i)
        rows_b = block_rows(nsub - 1 - i)
        gb = gb_ref[0, rows_b, :]
        p = prefix(jnp.concatenate([gf_ref[0, rows_f, :], gb], axis=1))
        tot = chunk_last(p)
        bf, totf = p[:, :DK], tot[:, :DK]
        pb, totb = p[:, DK:], tot[:, DK:]
        rb = totb - pb + gb
        o_f = direction(rows_f, bf, totf, mask_f, sf_ref, range(nc))
        o_b = direction(rows_b, rb, totb, mask_b, sb_ref, reversed(range(nc)))
        if first:
            oacc_ref[rows_f, :] = o_f
            oacc_ref[rows_b, :] = o_b
        else:
            finish(rows_f, oacc_ref[rows_f, :] + o_f)
            finish(rows_b, oacc_ref[rows_b, :] + o_b)

    sf_ref[...] = jnp.zeros_like(sf_ref)
    sb_ref[...] = jnp.zeros_like(sb_ref)
    half = nsub // 2
    unroll = 4 if half % 4 == 0 else (2 if half % 2 == 0 else 1)
    lax.fori_loop(0, half, lambda i, c: (pair(i, True), c)[1], 0, unroll=unroll)
    lax.fori_loop(half, nsub, lambda i, c: (pair(i, False), c)[1], 0, unroll=unroll)


def _gla_core(q, k, v, gf, gb, r, gn):
    B, L, _ = q.shape
    T = min(GLA_SUB, L)
    assert L % (2 * T) == 0 and T % GLA_CHUNK == 0
    qk = pl.BlockSpec((1, L, GLA_DK), lambda b, h: (b, 0, h))
    vv = pl.BlockSpec((1, L, GLA_DV), lambda b, h: (b, 0, h))
    return pl.pallas_call(
        functools.partial(_gla_core_kernel, L=L, T=T),
        grid=(B, GLA_HEADS),
        in_specs=[qk, qk, vv, qk, qk, vv, _const_spec((1, GLA_DV))],
        out_specs=vv,
        out_shape=jax.ShapeDtypeStruct((B, L, GLA_V_W), BF16),
        scratch_shapes=[pltpu.VMEM((L, GLA_DV), F32),
                        pltpu.VMEM((GLA_DK, GLA_DV), F32),
                        pltpu.VMEM((GLA_DK, GLA_DV), F32)],
        compiler_params=_params("arbitrary", "arbitrary"),
        name="gla_core",
    )(q, k, v, gf, gb, r, gn)


def _mla_proj_kernel(x_ref, g_ref, win_ref, qn_ref, kvn_ref, wuq_ref, wukv_ref, cos_ref, sin_ref,
                     q_ref, k_ref, vt_ref):
    H = MLA_HEADS
    xn = _rms(x_ref[...], g_ref[...]).astype(BF16)
    h = _dot(xn, win_ref[...])
    cq = h[:, :MLA_Q_RANK]
    ckv = h[:, MLA_Q_RANK:MLA_Q_RANK + MLA_KV_RANK]
    o = MLA_Q_RANK + MLA_KV_RANK
    cos = cos_ref[...]
    sin = sin_ref[...]
    kr = (h[:, o:o + LANES] * cos + h[:, o + LANES:o + 2 * LANES] * sin).astype(BF16)
    qa = _dot(_rms(cq, qn_ref[...]).astype(BF16), wuq_ref[...])
    kv = _dot(_rms(ckv, kvn_ref[...]).astype(BF16), wukv_ref[...])
    scale = (MLA_NOPE + MLA_ROPE) ** -0.5 * LOG2_E
    for hd in range(H):
        nope = qa[:, hd * LANES:(hd + 1) * LANES]
        rp = qa[:, (H + hd) * LANES:(H + hd + 1) * LANES]
        rs = qa[:, (2 * H + hd) * LANES:(2 * H + hd + 1) * LANES]
        q_ref[:, hd * MLA_QK_PAD:hd * MLA_QK_PAD + LANES] = (nope * scale).astype(BF16)
        q_ref[:, hd * MLA_QK_PAD + LANES:(hd + 1) * MLA_QK_PAD] = ((rp * cos + rs * sin) * scale).astype(BF16)
        k_ref[:, hd * MLA_QK_PAD:hd * MLA_QK_PAD + LANES] = kv[:, hd * LANES:(hd + 1) * LANES].astype(BF16)
        k_ref[:, hd * MLA_QK_PAD + LANES:(hd + 1) * MLA_QK_PAD] = kr
    vt_ref[0] = jnp.transpose(kv[:, H * LANES:]).astype(BF16)


def _mla_proj(x, g, win, qn, kvn, wuq, wukv, cos, sin, *, S, tm=512):
    T, D = x.shape
    tm = _row_tile(S, tm)
    ns = S // tm
    row = lambda n: pl.BlockSpec((tm, n), lambda i: (i, 0))
    pos = pl.BlockSpec((tm, LANES), lambda i: (i % ns, 0))
    QW = MLA_HEADS * MLA_QK_PAD
    VW = MLA_HEADS * MLA_V
    return pl.pallas_call(
        _mla_proj_kernel,
        grid=(T // tm,),
        in_specs=[row(D), _const_spec((1, D)), _const_spec(win.shape), _const_spec(qn.shape),
                  _const_spec(kvn.shape), _const_spec(wuq.shape), _const_spec(wukv.shape), pos, pos],
        out_specs=[row(QW), row(QW), pl.BlockSpec((1, VW, tm), lambda i: (i // ns, 0, i % ns))],
        out_shape=[jax.ShapeDtypeStruct((T, QW), BF16),
                   jax.ShapeDtypeStruct((T, QW), BF16),
                   jax.ShapeDtypeStruct((T // S, VW, S), BF16)],
        compiler_params=_params("arbitrary"),
        name="mla_proj",
    )(x, g, win, qn, kvn, wuq, wukv, cos, sin)


def _mla_attn_kernel(q_ref, k_ref, vt_ref, o_ref, s0_ref, s1_ref, m0_ref, m1_ref, *, nq):
    i = pl.program_id(2)
    s_refs = (s0_ref, s1_ref)
    m_refs = (m0_ref, m1_ref)

    S = k_ref.shape[1]
    kc = min(MLA_KEY_CHUNK, S)

    def step(score_slot, finish_slot):
        if score_slot is not None:
            qt = jnp.transpose(q_ref[0].astype(F32)).astype(BF16)
            m_new = None
        if finish_slot is not None:
            m_old = m_refs[finish_slot][...]
            l = None
            acc = None
        for j in range(S // kc):
            rows = slice(j * kc, (j + 1) * kc)
            if finish_slot is not None:
                p = jnp.exp2(s_refs[finish_slot][rows, :] - m_old)
                lj = jnp.sum(p, axis=0, keepdims=True)
                aj = _dot(vt_ref[0, :, rows], p.astype(BF16))
                l = lj if l is None else l + lj
                acc = aj if acc is None else acc + aj
            if score_slot is not None:
                st = _dot(k_ref[0, rows, :], qt)
                s_refs[score_slot][rows, :] = st
                mj = jnp.max(st, axis=0, keepdims=True)
                m_new = mj if m_new is None else jnp.maximum(m_new, mj)
        if score_slot is not None:
            m_refs[score_slot][...] = m_new
        if finish_slot is not None:
            o_ref[0] = jnp.transpose(acc / l).astype(BF16)

    @pl.when(i == 0)
    def _():
        step(0, None)

    for par in range(2):
        @pl.when((i > 0) & (i < nq) & (i % 2 == par))
        def _():
            step(par, 1 - par)

    @pl.when(i == nq)
    def _():
        step(None, (nq - 1) % 2)


def _mla_attn(q, k, v, *, tq=512):
    B, S, _ = q.shape
    tq = _row_tile(S, tq)
    nq = S // tq
    return pl.pallas_call(
        functools.partial(_mla_attn_kernel, nq=nq),
        grid=(B, MLA_HEADS, nq + 1),
        in_specs=[
            pl.BlockSpec((1, tq, MLA_QK_PAD), lambda b, h, i: (b, jnp.minimum(i, nq - 1), h)),
            pl.BlockSpec((1, S, MLA_QK_PAD), lambda b, h, i: (b, 0, h)),
            pl.BlockSpec((1, MLA_V, S), lambda b, h, i: (b, h, 0)),
        ],
        out_specs=pl.BlockSpec((1, tq, MLA_V), lambda b, h, i: (b, jnp.maximum(i - 1, 0), h)),
        out_shape=jax.ShapeDtypeStruct((B, S, MLA_HEADS * MLA_V), BF16),
        scratch_shapes=[pltpu.VMEM((S, tq), F32), pltpu.VMEM((S, tq), F32),
                        pltpu.VMEM((1, tq), F32), pltpu.VMEM((1, tq), F32)],
        compiler_params=_params("arbitrary", "arbitrary", "arbitrary"),
        name="mla_attn",
    )(q, k, v)


def _prep_weights(ffn_norm, ffn_w_in, ffn_w_out, mix_norm, ple_norm, ple_w_gate, ple_w_proj,
                  gla_w_in, gla_w_gf_up, gla_b_gf, gla_w_gb_up, gla_b_gb, gla_out_norm, gla_w_out,
                  mla_w_in, mla_q_norm, mla_kv_norm, mla_w_uq, mla_w_ukv, mla_w_out, final_norm):
    nck = D_FF // FFN_CHUNK
    w = {}
    gate = ffn_w_in[..., :D_FF].reshape(DEPTH, 2, D_MODEL, nck, FFN_CHUNK)
    up = ffn_w_in[..., D_FF:].reshape(DEPTH, 2, D_MODEL, nck, FFN_CHUNK)
    w["ffn_in"] = jnp.concatenate([gate, up], axis=-1).transpose(0, 1, 3, 2, 4).astype(BF16)
    w["ffn_out"] = ffn_w_out.reshape(DEPTH, 2, nck, FFN_CHUNK, D_MODEL).astype(BF16)
    w["ffn_norm"] = ffn_norm.reshape(DEPTH, 2, 1, D_MODEL)
    w["mix_norm"] = mix_norm.reshape(DEPTH, 1, D_MODEL)
    w["ple_norm"] = ple_norm.reshape(DEPTH, 1, D_MODEL)
    w["ple_gate"] = ple_w_gate.astype(BF16)
    w["ple_proj"] = ple_w_proj.astype(BF16)
    w["final_norm"] = final_norm.reshape(1, D_MODEL)

    main = 2 * GLA_QK_W + 2 * GLA_V_W
    NG = gla_w_in.shape[0]
    R = GLA_GATE_RANK
    w["gla_in"] = gla_w_in[..., :main].astype(BF16)
    w["gla_lo"] = jnp.pad(gla_w_in[..., main:], ((0, 0), (0, 0), (0, LANES - 2 * R))).astype(BF16)
    upm = jnp.zeros((NG, LANES, 2 * GLA_QK_W), F32)
    upm = upm.at[:, :R, :GLA_QK_W].set(gla_w_gf_up).at[:, R:2 * R, GLA_QK_W:].set(gla_w_gb_up)
    w["gla_up"] = upm.astype(BF16)
    w["gla_bup"] = jnp.concatenate([gla_b_gf, gla_b_gb], axis=-1).reshape(NG, 1, 2 * GLA_QK_W)
    w["gla_out_norm"] = gla_out_norm.reshape(NG, 1, GLA_DV)
    w["gla_out"] = gla_w_out.astype(BF16)

    NM = mla_w_in.shape[0]
    H = MLA_HEADS
    half = MLA_ROPE // 2
    padr = lambda t: jnp.pad(t, [(0, 0)] * (t.ndim - 1) + [(0, LANES - MLA_ROPE)])
    swap = lambda t: jnp.concatenate([t[..., half:], t[..., :half]], axis=-1)
    o = MLA_Q_RANK + MLA_KV_RANK
    kr = mla_w_in[..., o:]
    w["mla_in"] = jnp.concatenate([mla_w_in[..., :o], padr(kr), padr(swap(kr))], axis=-1).astype(BF16)
    uq = mla_w_uq.reshape(NM, MLA_Q_RANK, H, MLA_NOPE + MLA_ROPE)
    nope = uq[..., :MLA_NOPE].reshape(NM, MLA_Q_RANK, H * MLA_NOPE)
    rp = uq[..., MLA_NOPE:]
    w["mla_uq"] = jnp.concatenate(
        [nope, padr(rp).reshape(NM, MLA_Q_RANK, H * LANES), padr(swap(rp)).reshape(NM, MLA_Q_RANK, H * LANES)],
        axis=-1).astype(BF16)
    ukv = mla_w_ukv.reshape(NM, MLA_KV_RANK, H, MLA_NOPE + MLA_V)
    w["mla_ukv"] = jnp.concatenate(
        [ukv[..., :MLA_NOPE].reshape(NM, MLA_KV_RANK, H * MLA_NOPE),
         ukv[..., MLA_NOPE:].reshape(NM, MLA_KV_RANK, H * MLA_V)], axis=-1).astype(BF16)
    w["mla_q_norm"] = mla_q_norm.reshape(NM, 1, MLA_Q_RANK)
    w["mla_kv_norm"] = mla_kv_norm.reshape(NM, 1, MLA_KV_RANK)
    w["mla_out"] = mla_w_out.astype(BF16)
    return w


def _rope_tables(S):
    inv_freq = ROPE_THETA ** (-jnp.arange(0, MLA_ROPE, 2, dtype=F32) / MLA_ROPE)
    ang = jnp.arange(S, dtype=F32)[:, None] * inv_freq[None, :]
    cos, sin = jnp.cos(ang), jnp.sin(ang)
    z = jnp.zeros((S, LANES - MLA_ROPE), F32)
    return jnp.concatenate([cos, cos, z], axis=1), jnp.concatenate([-sin, sin, z], axis=1)


def _trunk(x, p, w):
    B, S, D = x.shape
    T = B * S
    x = x.reshape(T, D)
    p = p.reshape(DEPTH, T, PLE_DIM)
    cos, sin = _rope_tables(S)
    for i in range(DEPTH):
        j = i // 2
        x = _ffn(x, w, i, 0)
        if i % 2 == 0:
            q, k, v, r, gf, gb = _gla_proj(x, w["mix_norm"][i], w["gla_in"][j], w["gla_lo"][j],
                                           w["gla_up"][j], w["gla_bup"][j])
            sh = lambda t: t.reshape(B, S, t.shape[-1])
            o = _gla_core(sh(q), sh(k), sh(v), sh(gf), sh(gb), sh(r), w["gla_out_norm"][j])
            x = _post_mixer(x, o.reshape(T, GLA_V_W), w["gla_out"], p, w, i, j)
        else:
            q, k, vt = _mla_proj(x, w["mix_norm"][i], w["mla_in"][j], w["mla_q_norm"][j], w["mla_kv_norm"][j],
                                 w["mla_uq"][j], w["mla_ukv"][j], cos, sin, S=S)
            sh = lambda t: t.reshape(B, S, t.shape[-1])
            o = _mla_attn(sh(q), sh(k), vt)
            x = _post_mixer(x, o.reshape(T, MLA_HEADS * MLA_V), w["mla_out"], p, w, i, j)
    return x.reshape(B, S, D)


def kernel(x_prompt, x_sample, p_prompt, p_sample, ffn_norm, ffn_w_in, ffn_w_out, mix_norm, ple_norm,
           ple_w_gate, ple_w_proj, gla_w_in, gla_w_gf_up, gla_b_gf, gla_w_gb_up, gla_b_gb, gla_out_norm,
           gla_w_out, mla_w_in, mla_q_norm, mla_kv_norm, mla_w_uq, mla_w_ukv, mla_w_out, final_norm):
    w = _prep_weights(ffn_norm, ffn_w_in, ffn_w_out, mix_norm, ple_norm, ple_w_gate, ple_w_proj,
                      gla_w_in, gla_w_gf_up, gla_b_gf, gla_w_gb_up, gla_b_gb, gla_out_norm, gla_w_out,
                      mla_w_in, mla_q_norm, mla_kv_norm, mla_w_uq, mla_w_ukv, mla_w_out, final_norm)
    return (_trunk(x_prompt, p_prompt, w), _trunk(x_sample, p_sample, w))
```

```python
import functools

import jax
import jax.numpy as jnp
from jax import lax
from jax.experimental import pallas as pl
from jax.experimental.pallas import tpu as pltpu

F32 = jnp.float32
BF16 = jnp.bfloat16

D_MODEL = 1024
DEPTH = 2
PLE_DIM = 256
D_FF = 2816
NORM_EPS = 1e-6

GLA_HEADS = 4
GLA_DK = 128
GLA_DV = 256
GLA_GATE_RANK = 16
GLA_GATE_TAU = 16.0
GLA_CHUNK = 64
GLA_QK_W = GLA_HEADS * GLA_DK
GLA_V_W = GLA_HEADS * GLA_DV

MLA_HEADS = 8
MLA_Q_RANK = 384
MLA_KV_RANK = 256
MLA_NOPE = 128
MLA_ROPE = 64
MLA_V = 128
ROPE_THETA = 10000.0
LOG2_E = 1.4426950408889634

LANES = 128
MLA_QK_PAD = 2 * LANES
VMEM_LIMIT_BYTES = 56 * 1024 * 1024

FFN_CHUNK = 256
GLA_SUB = 256
MLA_KEY_CHUNK = 256


def _params(*sem):
    return pltpu.CompilerParams(dimension_semantics=sem, vmem_limit_bytes=VMEM_LIMIT_BYTES)


def _const_spec(shape):
    nd = len(shape)
    return pl.BlockSpec(shape, lambda *_: (0,) * nd)


def _row_tile(n, pref):
    t = min(pref, n)
    assert n % t == 0, (n, t)
    return t


def _rms(x, g):
    ms = jnp.mean(x * x, axis=-1, keepdims=True)
    return x * lax.rsqrt(ms + NORM_EPS) * g


def _dot(a, b):
    return jnp.dot(a, b, preferred_element_type=F32)


def _dot_nt(a, b):
    return lax.dot_general(a, b, (((1,), (1,)), ((), ())), preferred_element_type=F32)


def _dot_tn(a, b):
    return lax.dot_general(a, b, (((0,), (0,)), ((), ())), preferred_element_type=F32)


def _ffn_half_step(x, g_ref, win_ref, wout_ref, acc_ref):
    nck, _, ck2 = win_ref.shape
    ck = ck2 // 2
    xn = _rms(x, g_ref[...]).astype(BF16)

    def body(j, carry):
        h = _dot(xn, win_ref[j])
        gate = h[:, :ck]
        up = h[:, ck:]
        a = (gate * jax.nn.sigmoid(gate) * up).astype(BF16)
        acc_ref[...] += _dot(a, wout_ref[j])
        return carry

    acc_ref[...] = jnp.zeros_like(acc_ref)
    lax.fori_loop(0, nck, body, 0, unroll=True)
    return x + 0.5 * acc_ref[...]


def _ffn_kernel(x_ref, g_ref, win_ref, wout_ref, o_ref, acc_ref):
    o_ref[...] = _ffn_half_step(x_ref[...], g_ref, win_ref, wout_ref, acc_ref)


def _post_mixer_kernel(x_ref, a_ref, wo_ref, g_ref, win_ref, wout_ref, p_ref, gp_ref, wg_ref, wp_ref, gf_ref,
                       o_ref, acc_ref, *, final):
    x = x_ref[...] + _dot(a_ref[...], wo_ref[...])
    x = _ffn_half_step(x, g_ref, win_ref, wout_ref, acc_ref)
    gate = jax.nn.sigmoid(_dot(_rms(x, gp_ref[...]).astype(BF16), wg_ref[...]))
    x = x + gate * _dot(p_ref[...].astype(BF16), wp_ref[...])
    if final:
        x = _rms(x, gf_ref[...])
    o_ref[...] = x


def _layer_spec(arr, *idx):
    n = len(idx)
    rest = arr.shape[n:]
    return pl.BlockSpec((None,) * n + rest, lambda i: idx + (0,) * len(rest), pipeline_mode=pl.Buffered(1))


def _ffn(x, w, li, wi, *, tm=512):
    T, D = x.shape
    tm = _row_tile(T, tm)
    tok = lambda n: pl.BlockSpec((tm, n), lambda i: (i, 0))
    return pl.pallas_call(
        _ffn_kernel,
        grid=(T // tm,),
        in_specs=[tok(D), _layer_spec(w["ffn_norm"], li, wi), _layer_spec(w["ffn_in"], li, wi),
                  _layer_spec(w["ffn_out"], li, wi)],
        out_specs=tok(D),
        out_shape=jax.ShapeDtypeStruct((T, D), F32),
        scratch_shapes=[pltpu.VMEM((tm, D), F32)],
        compiler_params=_params("arbitrary"),
        name="ffn",
    )(x, w["ffn_norm"], w["ffn_in"], w["ffn_out"])


def _post_mixer(x, a, w_o, p, w, li, mi, *, tm=512):
    T, D = x.shape
    tm = _row_tile(T, tm)
    tok = lambda n: pl.BlockSpec((tm, n), lambda i: (i, 0))
    return pl.pallas_call(
        functools.partial(_post_mixer_kernel, final=(li == DEPTH - 1)),
        grid=(T // tm,),
        in_specs=[tok(D), tok(a.shape[1]), _layer_spec(w_o, mi),
                  _layer_spec(w["ffn_norm"], li, 1), _layer_spec(w["ffn_in"], li, 1), _layer_spec(w["ffn_out"], li, 1),
                  pl.BlockSpec((None, tm, PLE_DIM), lambda i: (li, i, 0)),
                  _layer_spec(w["ple_norm"], li), _layer_spec(w["ple_gate"], li), _layer_spec(w["ple_proj"], li),
                  _layer_spec(w["final_norm"])],
        out_specs=tok(D),
        out_shape=jax.ShapeDtypeStruct((T, D), F32),
        scratch_shapes=[pltpu.VMEM((tm, D), F32)],
        compiler_params=_params("arbitrary"),
        name="post_mixer",
    )(x, a, w_o, w["ffn_norm"], w["ffn_in"], w["ffn_out"], p, w["ple_norm"], w["ple_gate"], w["ple_proj"],
      w["final_norm"])


def _log_sigmoid(x):
    return jnp.minimum(x, 0.0) - jnp.log(1.0 + jnp.exp(-jnp.abs(x)))


def _gla_proj_kernel(x_ref, g_ref, w_ref, wlo_ref, wup_ref, bup_ref,
                     q_ref, k_ref, v_ref, r_ref, gf_ref, gb_ref):
    xn = _rms(x_ref[...], g_ref[...]).astype(BF16)
    lo = _dot(xn, wlo_ref[...]).astype(BF16)
    W, V = GLA_QK_W, GLA_V_W

    def gate(cols):
        return _log_sigmoid(_dot(lo, wup_ref[:, cols]) + bup_ref[:, cols]) / GLA_GATE_TAU

    q_ref[...] = _dot(xn, w_ref[:, :W]) * (GLA_DK ** -0.5)
    gf_ref[...] = gate(slice(0, W))
    k_ref[...] = _dot(xn, w_ref[:, W:2 * W])
    v_ref[...] = _dot(xn, w_ref[:, 2 * W:2 * W + V]).astype(BF16)
    gb_ref[...] = gate(slice(W, 2 * W))
    r_ref[...] = _dot(xn, w_ref[:, 2 * W + V:])


def _gla_proj(x, g, w, wlo, wup, bup, *, tm=512):
    T, D = x.shape
    tm = _row_tile(T, tm)
    row = lambda n: pl.BlockSpec((tm, n), lambda i: (i, 0))
    return pl.pallas_call(
        _gla_proj_kernel,
        grid=(T // tm,),
        in_specs=[row(D), _const_spec((1, D)), _const_spec(w.shape), _const_spec(wlo.shape),
                  _const_spec(wup.shape), _const_spec(bup.shape)],
        out_specs=[row(GLA_QK_W), row(GLA_QK_W), row(GLA_V_W), row(GLA_V_W), row(GLA_QK_W), row(GLA_QK_W)],
        out_shape=[
            jax.ShapeDtypeStruct((T, GLA_QK_W), F32),
            jax.ShapeDtypeStruct((T, GLA_QK_W), F32),
            jax.ShapeDtypeStruct((T, GLA_V_W), BF16),
            jax.ShapeDtypeStruct((T, GLA_V_W), F32),
            jax.ShapeDtypeStruct((T, GLA_QK_W), F32),
            jax.ShapeDtypeStruct((T, GLA_QK_W), F32),
        ],
        compiler_params=_params("arbitrary"),
        name="gla_proj",
    )(x, g, w, wlo, wup, bup)


def _split2(g):
    hi = g.astype(BF16)
    lo = (g - hi.astype(F32)).astype(BF16)
    return hi, lo


def _gla_core_kernel(q_ref, k_ref, v_ref, gf_ref, gb_ref, r_ref, gn_ref, o_ref,
                     oacc_ref, sf_ref, sb_ref, *, L, T):
    C = GLA_CHUNK
    nc = T // C
    nsub = L // T
    DK = GLA_DK

    row = lax.broadcasted_iota(jnp.int32, (T, T), 0)
    col = lax.broadcasted_iota(jnp.int32, (T, T), 1)
    cbits = C.bit_length() - 1
    same = lax.shift_right_logical(row, cbits) == lax.shift_right_logical(col, cbits)
    mask_f = same & (col <= row)
    mask_b = same & (col > row)
    tri = jnp.where(mask_f, 1.0, 0.0).astype(BF16)

    def chunk_last(x):
        n = x.shape[1]
        x3 = x.reshape(nc, C, n)
        return jnp.broadcast_to(x3[:, C - 1:C, :], (nc, C, n)).reshape(T, n)

    def prefix(g):
        hi, lo = _split2(g)
        return _dot(tri, hi) + _dot(tri, lo)

    def decay_cols(tot_row):
        return jnp.transpose(jnp.broadcast_to(jnp.exp(tot_row), (DK, DK)))

    def apply_decay(s, dec):
        return jnp.concatenate([s[:, :DK] * dec, s[:, DK:] * dec], axis=1)

    def block_rows(t):
        return pl.ds(pl.multiple_of(t * T, T), T)

    def direction(rows, lg, tot, mask, s_ref, order):
        q = q_ref[0, rows, :]
        k = k_ref[0, rows, :]
        v = v_ref[0, rows, :]
        qd = (q * jnp.exp(lg)).astype(BF16)
        ki = (k * jnp.exp(-lg)).astype(BF16)
        ke = (k * jnp.exp(tot - lg)).astype(BF16)
        a = jnp.where(mask, _dot_nt(qd, ki), 0.0).astype(BF16)
        o = _dot(a, v)
        outs = [None] * nc
        for c in order:
            sl = slice(c * C, (c + 1) * C)
            s = s_ref[...]
            outs[c] = o[sl] + _dot(qd[sl], s.astype(BF16))
            dec = decay_cols(tot[c * C:c * C + 1, :])
            s_ref[...] = apply_decay(s, dec) + _dot_tn(ke[sl], v[sl])
        return jnp.concatenate(outs, axis=0)

    gn = gn_ref[...]

    def finish(rows, o):
        r = r_ref[0, rows, :]
        o_ref[0, rows, :] = (_rms(o, gn) * (r * jax.nn.sigmoid(r))).astype(BF16)

    def pair(i, first):
        rows_f = block_rows(i)
        rows_b = block_rows(nsub - 1 - i)
        gb = gb_ref[0, rows_b, :]
        p = prefix(jnp.concatenate([gf_ref[0, rows_f, :], gb], axis=1))
        tot = chunk_last(p)
        bf, totf = p[:, :DK], tot[:, :DK]
        pb, totb = p[:, DK:], tot[:, DK:]
        rb = totb - pb + gb
        o_f = direction(rows_f, bf, totf, mask_f, sf_ref, range(nc))
        o_b = direction(rows_b, rb, totb, mask_b, sb_ref, reversed(range(nc)))
        if first:
            oacc_ref[rows_f, :] = o_f
            oacc_ref[rows_b, :] = o_b
        else:
            finish(rows_f, oacc_ref[rows_f, :] + o_f)
            finish(rows_b, oacc_ref[rows_b, :] + o_b)

    sf_ref[...] = jnp.zeros_like(sf_ref)
    sb_ref[...] = jnp.zeros_like(sb_ref)
    half = nsub // 2
    unroll = 4 if half % 4 == 0 else (2 if half % 2 == 0 else 1)
    lax.fori_loop(0, half, lambda i, c: (pair(i, True), c)[1], 0, unroll=unroll)
    lax.fori_loop(half, nsub, lambda i, c: (pair(i, False), c)[1], 0, unroll=unroll)


def _gla_core(q, k, v, gf, gb, r, gn):
    B, L, _ = q.shape
    T = min(GLA_SUB, L)
    assert L % (2 * T) == 0 and T % GLA_CHUNK == 0
    qk = pl.BlockSpec((1, L, GLA_DK), lambda b, h: (b, 0, h))
    vv = pl.BlockSpec((1, L, GLA_DV), lambda b, h: (b, 0, h))
    return pl.pallas_call(
        functools.partial(_gla_core_kernel, L=L, T=T),
        grid=(B, GLA_HEADS),
        in_specs=[qk, qk, vv, qk, qk, vv, _const_spec((1, GLA_DV))],
        out_specs=vv,
        out_shape=jax.ShapeDtypeStruct((B, L, GLA_V_W), BF16),
        scratch_shapes=[pltpu.VMEM((L, GLA_DV), F32),
                        pltpu.VMEM((GLA_DK, GLA_DV), F32),
                        pltpu.VMEM((GLA_DK, GLA_DV), F32)],
        compiler_params=_params("arbitrary", "arbitrary"),
        name="gla_core",
    )(q, k, v, gf, gb, r, gn)


def _mla_proj_kernel(x_ref, g_ref, win_ref, qn_ref, kvn_ref, wuq_ref, wukv_ref, cos_ref, sin_ref,
                     q_ref, k_ref, vt_ref):
    H = MLA_HEADS
    xn = _rms(x_ref[...], g_ref[...]).astype(BF16)
    h = _dot(xn, win_ref[...])
    cq = h[:, :MLA_Q_RANK]
    ckv = h[:, MLA_Q_RANK:MLA_Q_RANK + MLA_KV_RANK]
    o = MLA_Q_RANK + MLA_KV_RANK
    cos = cos_ref[...]
    sin = sin_ref[...]
    kr = (h[:, o:o + LANES] * cos + h[:, o + LANES:o + 2 * LANES] * sin).astype(BF16)
    qa = _dot(_rms(cq, qn_ref[...]).astype(BF16), wuq_ref[...])
    kv = _dot(_rms(ckv, kvn_ref[...]).astype(BF16), wukv_ref[...])
    scale = (MLA_NOPE + MLA_ROPE) ** -0.5 * LOG2_E
    for hd in range(H):
        nope = qa[:, hd * LANES:(hd + 1) * LANES]
        rp = qa[:, (H + hd) * LANES:(H + hd + 1) * LANES]
        rs = qa[:, (2 * H + hd) * LANES:(2 * H + hd + 1) * LANES]
        q_ref[:, hd * MLA_QK_PAD:hd * MLA_QK_PAD + LANES] = (nope * scale).astype(BF16)
        q_ref[:, hd * MLA_QK_PAD + LANES:(hd + 1) * MLA_QK_PAD] = ((rp * cos + rs * sin) * scale).astype(BF16)
        k_ref[:, hd * MLA_QK_PAD:hd * MLA_QK_PAD + LANES] = kv[:, hd * LANES:(hd + 1) * LANES].astype(BF16)
        k_ref[:, hd * MLA_QK_PAD + LANES:(hd + 1) * MLA_QK_PAD] = kr
    vt_ref[0] = jnp.transpose(kv[:, H * LANES:]).astype(BF16)


def _mla_proj(x, g, win, qn, kvn, wuq, wukv, cos, sin, *, S, tm=512):
    T, D = x.shape
    tm = _row_tile(S, tm)
    ns = S // tm
    row = lambda n: pl.BlockSpec((tm, n), lambda i: (i, 0))
    pos = pl.BlockSpec((tm, LANES), lambda i: (i % ns, 0))
    QW = MLA_HEADS * MLA_QK_PAD
    VW = MLA_HEADS * MLA_V
    return pl.pallas_call(
        _mla_proj_kernel,
        grid=(T // tm,),
        in_specs=[row(D), _const_spec((1, D)), _const_spec(win.shape), _const_spec(qn.shape),
                  _const_spec(kvn.shape), _const_spec(wuq.shape), _const_spec(wukv.shape), pos, pos],
        out_specs=[row(QW), row(QW), pl.BlockSpec((1, VW, tm), lambda i: (i // ns, 0, i % ns))],
        out_shape=[jax.ShapeDtypeStruct((T, QW), BF16),
                   jax.ShapeDtypeStruct((T, QW), BF16),
                   jax.ShapeDtypeStruct((T // S, VW, S), BF16)],
        compiler_params=_params("arbitrary"),
        name="mla_proj",
    )(x, g, win, qn, kvn, wuq, wukv, cos, sin)


def _mla_attn_kernel(q_ref, k_ref, vt_ref, o_ref, s0_ref, s1_ref, m0_ref, m1_ref, *, nt):
    i = pl.program_id(0)
    s_refs = (s0_ref, s1_ref)
    m_refs = (m0_ref, m1_ref)

    S = k_ref.shape[1]
    kc = min(MLA_KEY_CHUNK, S)

    def step(score_slot, finish_slot):
        n = S // kc
        state = {}

        def score_chunk(j):
            rows = slice(j * kc, (j + 1) * kc)
            if j == 0:
                state["qt"] = jnp.transpose(q_ref[0].astype(F32)).astype(BF16)
            st = _dot(k_ref[0, rows, :], state["qt"])
            s_refs[score_slot][rows, :] = st
            mj = jnp.max(st, axis=0, keepdims=True)
            state["m"] = mj if j == 0 else jnp.maximum(state["m"], mj)
            if j == n - 1:
                m_refs[score_slot][...] = state["m"]

        def finish_chunk(j):
            rows = slice(j * kc, (j + 1) * kc)
            p = jnp.exp2(s_refs[finish_slot][rows, :] - m_refs[finish_slot][...])
            lj = jnp.sum(p, axis=0, keepdims=True)
            aj = _dot(vt_ref[0, :, rows], p.astype(BF16))
            state["l"] = lj if j == 0 else state["l"] + lj
            state["acc"] = aj if j == 0 else state["acc"] + aj
            if j == n - 1:
                o_ref[0] = jnp.transpose(state["acc"] / state["l"]).astype(BF16)

        if finish_slot is None:
            order = [("s", j) for j in range(n)]
        elif score_slot is None:
            order = [("f", j) for j in range(n)]
        else:
            order = [("s", 0), ("f", 0)]
            for j in range(1, n):
                order += [("f", j), ("s", j)]
        for kind, j in order:
            (score_chunk if kind == "s" else finish_chunk)(j)

    @pl.when(i == 0)
    def _():
        step(0, None)

    for par in range(2):
        @pl.when((i > 0) & (i < nt) & (i % 2 == par))
        def _():
            step(par, 1 - par)

    @pl.when(i == nt)
    def _():
        step(None, (nt - 1) % 2)


def _mla_attn(q, k, v, *, tq=512):
    B, S, _ = q.shape
    tq = _row_tile(S, tq)
    nq = S // tq
    nt = B * MLA_HEADS * nq

    def tile(t):
        bh = t // nq
        return bh // MLA_HEADS, bh % MLA_HEADS, t % nq

    def scored(t):
        return tile(jnp.minimum(t, nt - 1))

    def finished(t):
        return tile(jnp.maximum(t - 1, 0))

    def q_map(t):
        b, h, i = scored(t)
        return b, i, h

    def k_map(t):
        b, h, _ = scored(t)
        return b, 0, h

    def v_map(t):
        b, h, _ = finished(t)
        return b, h, 0

    def o_map(t):
        b, h, i = finished(t)
        return b, i, h

    return pl.pallas_call(
        functools.partial(_mla_attn_kernel, nt=nt),
        grid=(nt + 1,),
        in_specs=[
            pl.BlockSpec((1, tq, MLA_QK_PAD), q_map),
            pl.BlockSpec((1, S, MLA_QK_PAD), k_map),
            pl.BlockSpec((1, MLA_V, S), v_map),
        ],
        out_specs=pl.BlockSpec((1, tq, MLA_V), o_map),
        out_shape=jax.ShapeDtypeStruct((B, S, MLA_HEADS * MLA_V), BF16),
        scratch_shapes=[pltpu.VMEM((S, tq), F32), pltpu.VMEM((S, tq), F32),
                        pltpu.VMEM((1, tq), F32), pltpu.VMEM((1, tq), F32)],
        compiler_params=_params("arbitrary"),
        name="mla_attn",
    )(q, k, v)


def _prep_weights(ffn_norm, ffn_w_in, ffn_w_out, mix_norm, ple_norm, ple_w_gate, ple_w_proj,
                  gla_w_in, gla_w_gf_up, gla_b_gf, gla_w_gb_up, gla_b_gb, gla_out_norm, gla_w_out,
                  mla_w_in, mla_q_norm, mla_kv_norm, mla_w_uq, mla_w_ukv, mla_w_out, final_norm):
    nck = D_FF // FFN_CHUNK
    w = {}
    gate = ffn_w_in[..., :D_FF].reshape(DEPTH, 2, D_MODEL, nck, FFN_CHUNK)
    up = ffn_w_in[..., D_FF:].reshape(DEPTH, 2, D_MODEL, nck, FFN_CHUNK)
    w["ffn_in"] = jnp.concatenate([gate, up], axis=-1).transpose(0, 1, 3, 2, 4).astype(BF16)
    w["ffn_out"] = ffn_w_out.reshape(DEPTH, 2, nck, FFN_CHUNK, D_MODEL).astype(BF16)
    w["ffn_norm"] = ffn_norm.reshape(DEPTH, 2, 1, D_MODEL)
    w["mix_norm"] = mix_norm.reshape(DEPTH, 1, D_MODEL)
    w["ple_norm"] = ple_norm.reshape(DEPTH, 1, D_MODEL)
    w["ple_gate"] = ple_w_gate.astype(BF16)
    w["ple_proj"] = ple_w_proj.astype(BF16)
    w["final_norm"] = final_norm.reshape(1, D_MODEL)

    main = 2 * GLA_QK_W + 2 * GLA_V_W
    NG = gla_w_in.shape[0]
    R = GLA_GATE_RANK
    w["gla_in"] = gla_w_in[..., :main].astype(BF16)
    w["gla_lo"] = jnp.pad(gla_w_in[..., main:], ((0, 0), (0, 0), (0, LANES - 2 * R))).astype(BF16)
    upm = jnp.zeros((NG, LANES, 2 * GLA_QK_W), F32)
    upm = upm.at[:, :R, :GLA_QK_W].set(gla_w_gf_up).at[:, R:2 * R, GLA_QK_W:].set(gla_w_gb_up)
    w["gla_up"] = upm.astype(BF16)
    w["gla_bup"] = jnp.concatenate([gla_b_gf, gla_b_gb], axis=-1).reshape(NG, 1, 2 * GLA_QK_W)
    w["gla_out_norm"] = gla_out_norm.reshape(NG, 1, GLA_DV)
    w["gla_out"] = gla_w_out.astype(BF16)

    NM = mla_w_in.shape[0]
    H = MLA_HEADS
    half = MLA_ROPE // 2
    padr = lambda t: jnp.pad(t, [(0, 0)] * (t.ndim - 1) + [(0, LANES - MLA_ROPE)])
    swap = lambda t: jnp.concatenate([t[..., half:], t[..., :half]], axis=-1)
    o = MLA_Q_RANK + MLA_KV_RANK
    kr = mla_w_in[..., o:]
    w["mla_in"] = jnp.concatenate([mla_w_in[..., :o], padr(kr), padr(swap(kr))], axis=-1).astype(BF16)
    uq = mla_w_uq.reshape(NM, MLA_Q_RANK, H, MLA_NOPE + MLA_ROPE)
    nope = uq[..., :MLA_NOPE].reshape(NM, MLA_Q_RANK, H * MLA_NOPE)
    rp = uq[..., MLA_NOPE:]
    w["mla_uq"] = jnp.concatenate(
        [nope, padr(rp).reshape(NM, MLA_Q_RANK, H * LANES), padr(swap(rp)).reshape(NM, MLA_Q_RANK, H * LANES)],
        axis=-1).astype(BF16)
    ukv = mla_w_ukv.reshape(NM, MLA_KV_RANK, H, MLA_NOPE + MLA_V)
    w["mla_ukv"] = jnp.concatenate(
        [ukv[..., :MLA_NOPE].reshape(NM, MLA_KV_RANK, H * MLA_NOPE),
         ukv[..., MLA_NOPE:].reshape(NM, MLA_KV_RANK, H * MLA_V)], axis=-1).astype(BF16)
    w["mla_q_norm"] = mla_q_norm.reshape(NM, 1, MLA_Q_RANK)
    w["mla_kv_norm"] = mla_kv_norm.reshape(NM, 1, MLA_KV_RANK)
    w["mla_out"] = mla_w_out.astype(BF16)
    return w


def _rope_tables(S):
    inv_freq = ROPE_THETA ** (-jnp.arange(0, MLA_ROPE, 2, dtype=F32) / MLA_ROPE)
    ang = jnp.arange(S, dtype=F32)[:, None] * inv_freq[None, :]
    cos, sin = jnp.cos(ang), jnp.sin(ang)
    z = jnp.zeros((S, LANES - MLA_ROPE), F32)
    return jnp.concatenate([cos, cos, z], axis=1), jnp.concatenate([-sin, sin, z], axis=1)


def _trunk(x, p, w):
    B, S, D = x.shape
    T = B * S
    x = x.reshape(T, D)
    p = p.reshape(DEPTH, T, PLE_DIM)
    cos, sin = _rope_tables(S)
    for i in range(DEPTH):
        j = i // 2
        x = _ffn(x, w, i, 0)
        if i % 2 == 0:
            q, k, v, r, gf, gb = _gla_proj(x, w["mix_norm"][i], w["gla_in"][j], w["gla_lo"][j],
                                           w["gla_up"][j], w["gla_bup"][j])
            sh = lambda t: t.reshape(B, S, t.shape[-1])
            o = _gla_core(sh(q), sh(k), sh(v), sh(gf), sh(gb), sh(r), w["gla_out_norm"][j])
            x = _post_mixer(x, o.reshape(T, GLA_V_W), w["gla_out"], p, w, i, j)
        else:
            q, k, vt = _mla_proj(x, w["mix_norm"][i], w["mla_in"][j], w["mla_q_norm"][j], w["mla_kv_norm"][j],
                                 w["mla_uq"][j], w["mla_ukv"][j], cos, sin, S=S)
            sh = lambda t: t.reshape(B, S, t.shape[-1])
            o = _mla_attn(sh(q), sh(k), vt)
            x = _post_mixer(x, o.reshape(T, MLA_HEADS * MLA_V), w["mla_out"], p, w, i, j)
    return x.reshape(B, S, D)


def kernel(x_prompt, x_sample, p_prompt, p_sample, ffn_norm, ffn_w_in, ffn_w_out, mix_norm, ple_norm,
           ple_w_gate, ple_w_proj, gla_w_in, gla_w_gf_up, gla_b_gf, gla_w_gb_up, gla_b_gb, gla_out_norm,
           gla_w_out, mla_w_in, mla_q_norm, mla_kv_norm, mla_w_uq, mla_w_ukv, mla_w_out, final_norm):
    w = _prep_weights(ffn_norm, ffn_w_in, ffn_w_out, mix_norm, ple_norm, ple_w_gate, ple_w_proj,
                      gla_w_in, gla_w_gf_up, gla_b_gf, gla_w_gb_up, gla_b_gb, gla_out_norm, gla_w_out,
                      mla_w_in, mla_q_norm, mla_kv_norm, mla_w_uq, mla_w_ukv, mla_w_out, final_norm)
    return (_trunk(x_prompt, p_prompt, w), _trunk(x_sample, p_sample, w))
```

```python
import functools

import jax
import jax.numpy as jnp
from jax import lax
from jax.experimental import pallas as pl
from jax.experimental.pallas import tpu as pltpu

F32 = jnp.float32
BF16 = jnp.bfloat16

D_MODEL = 1024
DEPTH = 2
PLE_DIM = 256
D_FF = 2816
NORM_EPS = 1e-6

GLA_HEADS = 4
GLA_DK = 128
GLA_DV = 256
GLA_GATE_RANK = 16
GLA_GATE_TAU = 16.0
GLA_CHUNK = 64
GLA_QK_W = GLA_HEADS * GLA_DK
GLA_V_W = GLA_HEADS * GLA_DV

MLA_HEADS = 8
MLA_Q_RANK = 384
MLA_KV_RANK = 256
MLA_NOPE = 128
MLA_ROPE = 64
MLA_V = 128
ROPE_THETA = 10000.0
LOG2_E = 1.4426950408889634

LANES = 128
MLA_QK_PAD = 2 * LANES
VMEM_LIMIT_BYTES = 56 * 1024 * 1024

FFN_CHUNK = 256
GLA_SUB = 256
MLA_KEY_CHUNK = 256


def _params(*sem):
    return pltpu.CompilerParams(dimension_semantics=sem, vmem_limit_bytes=VMEM_LIMIT_BYTES)


def _const_spec(shape):
    nd = len(shape)
    return pl.BlockSpec(shape, lambda *_: (0,) * nd)


def _row_tile(n, pref):
    t = min(pref, n)
    assert n % t == 0, (n, t)
    return t


def _rms(x, g):
    ms = jnp.mean(x * x, axis=-1, keepdims=True)
    return x * lax.rsqrt(ms + NORM_EPS) * g


def _dot(a, b):
    return jnp.dot(a, b, preferred_element_type=F32)


def _dot_nt(a, b):
    return lax.dot_general(a, b, (((1,), (1,)), ((), ())), preferred_element_type=F32)


def _dot_tn(a, b):
    return lax.dot_general(a, b, (((0,), (0,)), ((), ())), preferred_element_type=F32)


def _ffn_half_step(x, g_ref, win_ref, wout_ref, acc_ref):
    nck, ck, _ = wout_ref.shape
    d_ff = nck * ck
    xn = _rms(x, g_ref[...]).astype(BF16)
    acc_ref[...] = jnp.zeros_like(acc_ref)
    for j in range(nck):
        gate = _dot(xn, win_ref[:, j * ck:(j + 1) * ck])
        up = _dot(xn, win_ref[:, d_ff + j * ck:d_ff + (j + 1) * ck])
        a = (gate * jax.nn.sigmoid(gate) * up).astype(BF16)
        acc_ref[...] += _dot(a, wout_ref[j])
    return x + 0.5 * acc_ref[...]


def _ffn_kernel(x_ref, g_ref, win_ref, wout_ref, o_ref, acc_ref):
    o_ref[...] = _ffn_half_step(x_ref[...], g_ref, win_ref, wout_ref, acc_ref)


def _post_mixer_kernel(x_ref, a_ref, wo_ref, g_ref, win_ref, wout_ref, p_ref, gp_ref, wg_ref, wp_ref, gf_ref,
                       o_ref, acc_ref, *, final):
    x = x_ref[...] + _dot(a_ref[...], wo_ref[...])
    x = _ffn_half_step(x, g_ref, win_ref, wout_ref, acc_ref)
    gate = jax.nn.sigmoid(_dot(_rms(x, gp_ref[...]).astype(BF16), wg_ref[...]))
    x = x + gate * _dot(p_ref[...].astype(BF16), wp_ref[...])
    if final:
        x = _rms(x, gf_ref[...])
    o_ref[...] = x


def _layer_spec(arr, *idx):
    n = len(idx)
    rest = arr.shape[n:]
    return pl.BlockSpec((None,) * n + rest, lambda i: idx + (0,) * len(rest), pipeline_mode=pl.Buffered(1))


def _ffn(x, w, li, wi, *, tm=512):
    T, D = x.shape
    tm = _row_tile(T, tm)
    tok = lambda n: pl.BlockSpec((tm, n), lambda i: (i, 0))
    return pl.pallas_call(
        _ffn_kernel,
        grid=(T // tm,),
        in_specs=[tok(D), _layer_spec(w["ffn_norm"], li, wi), _layer_spec(w["ffn_in"], li, wi),
                  _layer_spec(w["ffn_out"], li, wi)],
        out_specs=tok(D),
        out_shape=jax.ShapeDtypeStruct((T, D), F32),
        scratch_shapes=[pltpu.VMEM((tm, D), F32)],
        compiler_params=_params("arbitrary"),
        name="ffn",
    )(x, w["ffn_norm"], w["ffn_in"], w["ffn_out"])


def _post_mixer(x, a, w_o, p, w, li, mi, *, tm=512):
    T, D = x.shape
    tm = _row_tile(T, tm)
    tok = lambda n: pl.BlockSpec((tm, n), lambda i: (i, 0))
    return pl.pallas_call(
        functools.partial(_post_mixer_kernel, final=(li == DEPTH - 1)),
        grid=(T // tm,),
        in_specs=[tok(D), tok(a.shape[1]), _layer_spec(w_o, mi),
                  _layer_spec(w["ffn_norm"], li, 1), _layer_spec(w["ffn_in"], li, 1), _layer_spec(w["ffn_out"], li, 1),
                  pl.BlockSpec((None, tm, PLE_DIM), lambda i: (li, i, 0)),
                  _layer_spec(w["ple_norm"], li), _layer_spec(w["ple_gate"], li), _layer_spec(w["ple_proj"], li),
                  _layer_spec(w["final_norm"])],
        out_specs=tok(D),
        out_shape=jax.ShapeDtypeStruct((T, D), F32),
        scratch_shapes=[pltpu.VMEM((tm, D), F32)],
        compiler_params=_params("arbitrary"),
        name="post_mixer",
    )(x, a, w_o, w["ffn_norm"], w["ffn_in"], w["ffn_out"], p, w["ple_norm"], w["ple_gate"], w["ple_proj"],
      w["final_norm"])


def _log_sigmoid(x):
    return jnp.minimum(x, 0.0) - jnp.log(1.0 + jnp.exp(-jnp.abs(x)))


def _gla_proj_kernel(x_ref, g_ref, w_ref, wlo_ref, wup_ref, bup_ref,
                     q_ref, k_ref, v_ref, r_ref, gf_ref, gb_ref):
    xn = _rms(x_ref[...], g_ref[...]).astype(BF16)
    lo = _dot(xn, wlo_ref[...]).astype(BF16)
    W, V = GLA_QK_W, GLA_V_W

    def gate(cols):
        return _log_sigmoid(_dot(lo, wup_ref[:, cols]) + bup_ref[:, cols]) / GLA_GATE_TAU

    q_ref[...] = _dot(xn, w_ref[:, :W]) * (GLA_DK ** -0.5)
    gf_ref[...] = gate(slice(0, W))
    k_ref[...] = _dot(xn, w_ref[:, W:2 * W])
    v_ref[...] = _dot(xn, w_ref[:, 2 * W:2 * W + V]).astype(BF16)
    gb_ref[...] = gate(slice(W, 2 * W))
    r_ref[...] = _dot(xn, w_ref[:, 2 * W + V:])


def _gla_proj(x, g, w, wlo, wup, bup, *, tm=512):
    T, D = x.shape
    tm = _row_tile(T, tm)
    row = lambda n: pl.BlockSpec((tm, n), lambda i: (i, 0))
    return pl.pallas_call(
        _gla_proj_kernel,
        grid=(T // tm,),
        in_specs=[row(D), _const_spec((1, D)), _const_spec(w.shape), _const_spec(wlo.shape),
                  _const_spec(wup.shape), _const_spec(bup.shape)],
        out_specs=[row(GLA_QK_W), row(GLA_QK_W), row(GLA_V_W), row(GLA_V_W), row(GLA_QK_W), row(GLA_QK_W)],
        out_shape=[
            jax.ShapeDtypeStruct((T, GLA_QK_W), F32),
            jax.ShapeDtypeStruct((T, GLA_QK_W), F32),
            jax.ShapeDtypeStruct((T, GLA_V_W), BF16),
            jax.ShapeDtypeStruct((T, GLA_V_W), F32),
            jax.ShapeDtypeStruct((T, GLA_QK_W), F32),
            jax.ShapeDtypeStruct((T, GLA_QK_W), F32),
        ],
        compiler_params=_params("arbitrary"),
        name="gla_proj",
    )(x, g, w, wlo, wup, bup)


def _split2(g):
    hi = g.astype(BF16)
    lo = (g - hi.astype(F32)).astype(BF16)
    return hi, lo


def _gla_core_kernel(q_ref, k_ref, v_ref, gf_ref, gb_ref, r_ref, gn_ref, o_ref,
                     oacc_ref, sf_ref, sb_ref, *, L, T):
    C = GLA_CHUNK
    nc = T // C
    nsub = L // T
    DK = GLA_DK

    row = lax.broadcasted_iota(jnp.int32, (T, T), 0)
    col = lax.broadcasted_iota(jnp.int32, (T, T), 1)
    cbits = C.bit_length() - 1
    same = lax.shift_right_logical(row, cbits) == lax.shift_right_logical(col, cbits)
    mask_f = same & (col <= row)
    mask_b = same & (col > row)
    tri = jnp.where(mask_f, 1.0, 0.0).astype(BF16)

    def chunk_last(x):
        n = x.shape[1]
        x3 = x.reshape(nc, C, n)
        return jnp.broadcast_to(x3[:, C - 1:C, :], (nc, C, n)).reshape(T, n)

    def prefix(g):
        hi, lo = _split2(g)
        return _dot(tri, hi) + _dot(tri, lo)

    def decay_cols(tot_row):
        return jnp.transpose(jnp.broadcast_to(jnp.exp(tot_row), (DK, DK)))

    def apply_decay(s, dec):
        return jnp.concatenate([s[:, :DK] * dec, s[:, DK:] * dec], axis=1)

    def block_rows(t):
        return pl.ds(pl.multiple_of(t * T, T), T)

    def direction(rows, lg, tot, mask, s_ref, order):
        q = q_ref[0, rows, :]
        k = k_ref[0, rows, :]
        v = v_ref[0, rows, :]
        qd = (q * jnp.exp(lg)).astype(BF16)
        ki = (k * jnp.exp(-lg)).astype(BF16)
        ke = (k * jnp.exp(tot - lg)).astype(BF16)
        a = jnp.where(mask, _dot_nt(qd, ki), 0.0).astype(BF16)
        o = _dot(a, v)
        outs = [None] * nc
        for c in order:
            sl = slice(c * C, (c + 1) * C)
            s = s_ref[...]
            outs[c] = o[sl] + _dot(qd[sl], s.astype(BF16))
            dec = decay_cols(tot[c * C:c * C + 1, :])
            s_ref[...] = apply_decay(s, dec) + _dot_tn(ke[sl], v[sl])
        return jnp.concatenate(outs, axis=0)

    gn = gn_ref[...]

    def finish(rows, o):
        r = r_ref[0, rows, :]
        o_ref[0, rows, :] = (_rms(o, gn) * (r * jax.nn.sigmoid(r))).astype(BF16)

    def pair(i, first):
        rows_f = block_rows(i)
        rows_b = block_rows(nsub - 1 - i)
        gb = gb_ref[0, rows_b, :]
        p = prefix(jnp.concatenate([gf_ref[0, rows_f, :], gb], axis=1))
        tot = chunk_last(p)
        bf, totf = p[:, :DK], tot[:, :DK]
        pb, totb = p[:, DK:], tot[:, DK:]
        rb = totb - pb + gb
        o_f = direction(rows_f, bf, totf, mask_f, sf_ref, range(nc))
        o_b = direction(rows_b, rb, totb, mask_b, sb_ref, reversed(range(nc)))
        if first:
            oacc_ref[rows_f, :] = o_f
            oacc_ref[rows_b, :] = o_b
        else:
            finish(rows_f, oacc_ref[rows_f, :] + o_f)
            finish(rows_b, oacc_ref[rows_b, :] + o_b)

    sf_ref[...] = jnp.zeros_like(sf_ref)
    sb_ref[...] = jnp.zeros_like(sb_ref)
    half = nsub // 2
    unroll = next(u for u in (8, 4, 2, 1) if half % u == 0)
    lax.fori_loop(0, half, lambda i, c: (pair(i, True), c)[1], 0, unroll=unroll)
    lax.fori_loop(half, nsub, lambda i, c: (pair(i, False), c)[1], 0, unroll=unroll)


def _gla_core(q, k, v, gf, gb, r, gn):
    B, L, _ = q.shape
    T = min(GLA_SUB, L)
    assert L % (2 * T) == 0 and T % GLA_CHUNK == 0
    qk = pl.BlockSpec((1, L, GLA_DK), lambda b, h: (b, 0, h))
    vv = pl.BlockSpec((1, L, GLA_DV), lambda b, h: (b, 0, h))
    return pl.pallas_call(
        functools.partial(_gla_core_kernel, L=L, T=T),
        grid=(B, GLA_HEADS),
        in_specs=[qk, qk, vv, qk, qk, vv, _const_spec((1, GLA_DV))],
        out_specs=vv,
        out_shape=jax.ShapeDtypeStruct((B, L, GLA_V_W), BF16),
        scratch_shapes=[pltpu.VMEM((L, GLA_DV), F32),
                        pltpu.VMEM((GLA_DK, GLA_DV), F32),
                        pltpu.VMEM((GLA_DK, GLA_DV), F32)],
        compiler_params=_params("arbitrary", "arbitrary"),
        name="gla_core",
    )(q, k, v, gf, gb, r, gn)


def _mla_proj_kernel(x_ref, g_ref, win_ref, qn_ref, kvn_ref, wuq_ref, wukv_ref, cos_ref, sin_ref,
                     q_ref, k_ref, vt_ref):
    H = MLA_HEADS
    xn = _rms(x_ref[...], g_ref[...]).astype(BF16)
    h = _dot(xn, win_ref[...])
    cq = h[:, :MLA_Q_RANK]
    ckv = h[:, MLA_Q_RANK:MLA_Q_RANK + MLA_KV_RANK]
    o = MLA_Q_RANK + MLA_KV_RANK
    cos = cos_ref[...]
    sin = sin_ref[...]
    kr = (h[:, o:o + LANES] * cos + h[:, o + LANES:o + 2 * LANES] * sin).astype(BF16)
    qa = _dot(_rms(cq, qn_ref[...]).astype(BF16), wuq_ref[...])
    kv = _dot(_rms(ckv, kvn_ref[...]).astype(BF16), wukv_ref[...])
    scale = (MLA_NOPE + MLA_ROPE) ** -0.5 * LOG2_E
    for hd in range(H):
        nope = qa[:, hd * LANES:(hd + 1) * LANES]
        rp = qa[:, (H + hd) * LANES:(H + hd + 1) * LANES]
        rs = qa[:, (2 * H + hd) * LANES:(2 * H + hd + 1) * LANES]
        q_ref[:, hd * MLA_QK_PAD:hd * MLA_QK_PAD + LANES] = (nope * scale).astype(BF16)
        q_ref[:, hd * MLA_QK_PAD + LANES:(hd + 1) * MLA_QK_PAD] = ((rp * cos + rs * sin) * scale).astype(BF16)
        k_ref[:, hd * MLA_QK_PAD:hd * MLA_QK_PAD + LANES] = kv[:, hd * LANES:(hd + 1) * LANES].astype(BF16)
        k_ref[:, hd * MLA_QK_PAD + LANES:(hd + 1) * MLA_QK_PAD] = kr
    vt_ref[0] = jnp.transpose(kv[:, H * LANES:]).astype(BF16)


def _mla_proj(x, g, win, qn, kvn, wuq, wukv, cos, sin, *, S, tm=512):
    T, D = x.shape
    tm = _row_tile(S, tm)
    ns = S // tm
    row = lambda n: pl.BlockSpec((tm, n), lambda i: (i, 0))
    pos = pl.BlockSpec((tm, LANES), lambda i: (i % ns, 0))
    QW = MLA_HEADS * MLA_QK_PAD
    VW = MLA_HEADS * MLA_V
    return pl.pallas_call(
        _mla_proj_kernel,
        grid=(T // tm,),
        in_specs=[row(D), _const_spec((1, D)), _const_spec(win.shape), _const_spec(qn.shape),
                  _const_spec(kvn.shape), _const_spec(wuq.shape), _const_spec(wukv.shape), pos, pos],
        out_specs=[row(QW), row(QW), pl.BlockSpec((1, VW, tm), lambda i: (i // ns, 0, i % ns))],
        out_shape=[jax.ShapeDtypeStruct((T, QW), BF16),
                   jax.ShapeDtypeStruct((T, QW), BF16),
                   jax.ShapeDtypeStruct((T // S, VW, S), BF16)],
        compiler_params=_params("arbitrary"),
        name="mla_proj",
    )(x, g, win, qn, kvn, wuq, wukv, cos, sin)


def _mla_attn_kernel(q_ref, k_ref, vt_ref, o_ref, s0_ref, s1_ref, m0_ref, m1_ref, *, nt):
    i = pl.program_id(0)
    s_refs = (s0_ref, s1_ref)
    m_refs = (m0_ref, m1_ref)

    S = k_ref.shape[1]
    kc = min(MLA_KEY_CHUNK, S)

    def step(score_slot, finish_slot):
        n = S // kc
        state = {}

        def score_chunk(j):
            rows = slice(j * kc, (j + 1) * kc)
            if j == 0:
                state["qt"] = jnp.transpose(q_ref[0].astype(F32)).astype(BF16)
            st = _dot(k_ref[0, rows, :], state["qt"])
            s_refs[score_slot][rows, :] = st
            mj = jnp.max(st, axis=0, keepdims=True)
            state["m"] = mj if j == 0 else jnp.maximum(state["m"], mj)
            if j == n - 1:
                m_refs[score_slot][...] = state["m"]

        def finish_chunk(j):
            rows = slice(j * kc, (j + 1) * kc)
            p = jnp.exp2(s_refs[finish_slot][rows, :] - m_refs[finish_slot][...])
            lj = jnp.sum(p, axis=0, keepdims=True)
            aj = _dot(vt_ref[0, :, rows], p.astype(BF16))
            state["l"] = lj if j == 0 else state["l"] + lj
            state["acc"] = aj if j == 0 else state["acc"] + aj
            if j == n - 1:
                o_ref[0] = jnp.transpose(state["acc"] / state["l"]).astype(BF16)

        if finish_slot is None:
            order = [("s", j) for j in range(n)]
        elif score_slot is None:
            order = [("f", j) for j in range(n)]
        else:
            order = [("s", 0), ("f", 0)]
            for j in range(1, n):
                order += [("f", j), ("s", j)]
        for kind, j in order:
            (score_chunk if kind == "s" else finish_chunk)(j)

    @pl.when(i == 0)
    def _():
        step(0, None)

    for par in range(2):
        @pl.when((i > 0) & (i < nt) & (i % 2 == par))
        def _():
            step(par, 1 - par)

    @pl.when(i == nt)
    def _():
        step(None, (nt - 1) % 2)


def _mla_attn(q, k, v, *, tq=1024):
    B, S, _ = q.shape
    tq = _row_tile(S, tq)
    nq = S // tq
    nt = B * MLA_HEADS * nq

    def tile(t):
        bh = t // nq
        return bh // MLA_HEADS, bh % MLA_HEADS, t % nq

    def scored(t):
        return tile(jnp.minimum(t, nt - 1))

    def finished(t):
        return tile(jnp.maximum(t - 1, 0))

    def q_map(t):
        b, h, i = scored(t)
        return b, i, h

    def k_map(t):
        b, h, _ = scored(t)
        return b, 0, h

    def v_map(t):
        b, h, _ = finished(t)
        return b, h, 0

    def o_map(t):
        b, h, i = finished(t)
        return b, i, h

    return pl.pallas_call(
        functools.partial(_mla_attn_kernel, nt=nt),
        grid=(nt + 1,),
        in_specs=[
            pl.BlockSpec((1, tq, MLA_QK_PAD), q_map),
            pl.BlockSpec((1, S, MLA_QK_PAD), k_map),
            pl.BlockSpec((1, MLA_V, S), v_map),
        ],
        out_specs=pl.BlockSpec((1, tq, MLA_V), o_map),
        out_shape=jax.ShapeDtypeStruct((B, S, MLA_HEADS * MLA_V), BF16),
        scratch_shapes=[pltpu.VMEM((S, tq), F32), pltpu.VMEM((S, tq), F32),
                        pltpu.VMEM((1, tq), F32), pltpu.VMEM((1, tq), F32)],
        compiler_params=_params("arbitrary"),
        name="mla_attn",
    )(q, k, v)


def _prep_weights(ffn_norm, ffn_w_in, ffn_w_out, mix_norm, ple_norm, ple_w_gate, ple_w_proj,
                  gla_w_in, gla_w_gf_up, gla_b_gf, gla_w_gb_up, gla_b_gb, gla_out_norm, gla_w_out,
                  mla_w_in, mla_q_norm, mla_kv_norm, mla_w_uq, mla_w_ukv, mla_w_out, final_norm):
    nck = D_FF // FFN_CHUNK
    w = {}
    w["ffn_in"] = ffn_w_in.astype(BF16)
    w["ffn_out"] = ffn_w_out.reshape(DEPTH, 2, nck, FFN_CHUNK, D_MODEL).astype(BF16)
    w["ffn_norm"] = ffn_norm.reshape(DEPTH, 2, 1, D_MODEL)
    w["mix_norm"] = mix_norm.reshape(DEPTH, 1, D_MODEL)
    w["ple_norm"] = ple_norm.reshape(DEPTH, 1, D_MODEL)
    w["ple_gate"] = ple_w_gate.astype(BF16)
    w["ple_proj"] = ple_w_proj.astype(BF16)
    w["final_norm"] = final_norm.reshape(1, D_MODEL)

    main = 2 * GLA_QK_W + 2 * GLA_V_W
    NG = gla_w_in.shape[0]
    R = GLA_GATE_RANK
    w["gla_in"] = gla_w_in[..., :main].astype(BF16)
    w["gla_lo"] = jnp.pad(gla_w_in[..., main:], ((0, 0), (0, 0), (0, LANES - 2 * R))).astype(BF16)
    upm = jnp.zeros((NG, LANES, 2 * GLA_QK_W), F32)
    upm = upm.at[:, :R, :GLA_QK_W].set(gla_w_gf_up).at[:, R:2 * R, GLA_QK_W:].set(gla_w_gb_up)
    w["gla_up"] = upm.astype(BF16)
    w["gla_bup"] = jnp.concatenate([gla_b_gf, gla_b_gb], axis=-1).reshape(NG, 1, 2 * GLA_QK_W)
    w["gla_out_norm"] = gla_out_norm.reshape(NG, 1, GLA_DV)
    w["gla_out"] = gla_w_out.astype(BF16)

    NM = mla_w_in.shape[0]
    H = MLA_HEADS
    half = MLA_ROPE // 2
    padr = lambda t: jnp.pad(t, [(0, 0)] * (t.ndim - 1) + [(0, LANES - MLA_ROPE)])
    swap = lambda t: jnp.concatenate([t[..., half:], t[..., :half]], axis=-1)
    o = MLA_Q_RANK + MLA_KV_RANK
    kr = mla_w_in[..., o:]
    w["mla_in"] = jnp.concatenate([mla_w_in[..., :o], padr(kr), padr(swap(kr))], axis=-1).astype(BF16)
    uq = mla_w_uq.reshape(NM, MLA_Q_RANK, H, MLA_NOPE + MLA_ROPE)
    nope = uq[..., :MLA_NOPE].reshape(NM, MLA_Q_RANK, H * MLA_NOPE)
    rp = uq[..., MLA_NOPE:]
    w["mla_uq"] = jnp.concatenate(
        [nope, padr(rp).reshape(NM, MLA_Q_RANK, H * LANES), padr(swap(rp)).reshape(NM, MLA_Q_RANK, H * LANES)],
        axis=-1).astype(BF16)
    ukv = mla_w_ukv.reshape(NM, MLA_KV_RANK, H, MLA_NOPE + MLA_V)
    w["mla_ukv"] = jnp.concatenate(
        [ukv[..., :MLA_NOPE].reshape(NM, MLA_KV_RANK, H * MLA_NOPE),
         ukv[..., MLA_NOPE:].reshape(NM, MLA_KV_RANK, H * MLA_V)], axis=-1).astype(BF16)
    w["mla_q_norm"] = mla_q_norm.reshape(NM, 1, MLA_Q_RANK)
    w["mla_kv_norm"] = mla_kv_norm.reshape(NM, 1, MLA_KV_RANK)
    w["mla_out"] = mla_w_out.astype(BF16)
    return w


def _rope_tables(S):
    inv_freq = ROPE_THETA ** (-jnp.arange(0, MLA_ROPE, 2, dtype=F32) / MLA_ROPE)
    ang = jnp.arange(S, dtype=F32)[:, None] * inv_freq[None, :]
    cos, sin = jnp.cos(ang), jnp.sin(ang)
    z = jnp.zeros((S, LANES - MLA_ROPE), F32)
    return jnp.concatenate([cos, cos, z], axis=1), jnp.concatenate([-sin, sin, z], axis=1)


def _trunk(x, p, w):
    B, S, D = x.shape
    T = B * S
    x = x.reshape(T, D)
    p = p.reshape(DEPTH, T, PLE_DIM)
    cos, sin = _rope_tables(S)
    for i in range(DEPTH):
        j = i // 2
        x = _ffn(x, w, i, 0)
        if i % 2 == 0:
            q, k, v, r, gf, gb = _gla_proj(x, w["mix_norm"][i], w["gla_in"][j], w["gla_lo"][j],
                                           w["gla_up"][j], w["gla_bup"][j])
            sh = lambda t: t.reshape(B, S, t.shape[-1])
            o = _gla_core(sh(q), sh(k), sh(v), sh(gf), sh(gb), sh(r), w["gla_out_norm"][j])
            x = _post_mixer(x, o.reshape(T, GLA_V_W), w["gla_out"], p, w, i, j)
        else:
            q, k, vt = _mla_proj(x, w["mix_norm"][i], w["mla_in"][j], w["mla_q_norm"][j], w["mla_kv_norm"][j],
                                 w["mla_uq"][j], w["mla_ukv"][j], cos, sin, S=S)
            sh = lambda t: t.reshape(B, S, t.shape[-1])
            o = _mla_attn(sh(q), sh(k), vt)
            x = _post_mixer(x, o.reshape(T, MLA_HEADS * MLA_V), w["mla_out"], p, w, i, j)
    return x.reshape(B, S, D)


def kernel(x_prompt, x_sample, p_prompt, p_sample, ffn_norm, ffn_w_in, ffn_w_out, mix_norm, ple_norm,
           ple_w_gate, ple_w_proj, gla_w_in, gla_w_gf_up, gla_b_gf, gla_w_gb_up, gla_b_gb, gla_out_norm,
           gla_w_out, mla_w_in, mla_q_norm, mla_kv_norm, mla_w_uq, mla_w_ukv, mla_w_out, final_norm):
    w = _prep_weights(ffn_norm, ffn_w_in, ffn_w_out, mix_norm, ple_norm, ple_w_gate, ple_w_proj,
                      gla_w_in, gla_w_gf_up, gla_b_gf, gla_w_gb_up, gla_b_gb, gla_out_norm, gla_w_out,
                      mla_w_in, mla_q_norm, mla_kv_norm, mla_w_uq, mla_w_ukv, mla_w_out, final_norm)
    return (_trunk(x_prompt, p_prompt, w), _trunk(x_sample, p_sample, w))
```

```python
import functools

import jax
import jax.numpy as jnp
from jax import lax
from jax.experimental import pallas as pl
from jax.experimental.pallas import tpu as pltpu

F32 = jnp.float32
BF16 = jnp.bfloat16

D_MODEL = 1024
DEPTH = 2
PLE_DIM = 256
D_FF = 2816
NORM_EPS = 1e-6

GLA_HEADS = 4
GLA_DK = 128
GLA_DV = 256
GLA_GATE_RANK = 16
GLA_GATE_TAU = 16.0
GLA_CHUNK = 64
GLA_QK_W = GLA_HEADS * GLA_DK
GLA_V_W = GLA_HEADS * GLA_DV

MLA_HEADS = 8
MLA_Q_RANK = 384
MLA_KV_RANK = 256
MLA_NOPE = 128
MLA_ROPE = 64
MLA_V = 128
ROPE_THETA = 10000.0
LOG2_E = 1.4426950408889634

LANES = 128
MLA_QK_PAD = 2 * LANES
VMEM_LIMIT_BYTES = 56 * 1024 * 1024

FFN_CHUNK = 256
GLA_SUB = 256
MLA_KEY_CHUNK = 256


def _params(*sem):
    return pltpu.CompilerParams(dimension_semantics=sem, vmem_limit_bytes=VMEM_LIMIT_BYTES)


def _const_spec(shape):
    nd = len(shape)
    return pl.BlockSpec(shape, lambda *_: (0,) * nd)


def _row_tile(n, pref):
    t = min(pref, n)
    assert n % t == 0, (n, t)
    return t


def _rms(x, g):
    ms = jnp.mean(x * x, axis=-1, keepdims=True)
    return x * lax.rsqrt(ms + NORM_EPS) * g


def _dot(a, b):
    return jnp.dot(a, b, preferred_element_type=F32)


def _dot_nt(a, b):
    return lax.dot_general(a, b, (((1,), (1,)), ((), ())), preferred_element_type=F32)


def _dot_tn(a, b):
    return lax.dot_general(a, b, (((0,), (0,)), ((), ())), preferred_element_type=F32)


def _ffn_half_step(x, g_ref, win_ref, wout_ref, acc_ref):
    nck, ck, _ = wout_ref.shape
    d_ff = nck * ck
    xn = _rms(x, g_ref[...]).astype(BF16)
    acc_ref[...] = jnp.zeros_like(acc_ref)
    for j in range(nck):
        gate = _dot(xn, win_ref[:, j * ck:(j + 1) * ck])
        up = _dot(xn, win_ref[:, d_ff + j * ck:d_ff + (j + 1) * ck])
        a = (gate * jax.nn.sigmoid(gate) * up).astype(BF16)
        acc_ref[...] += _dot(a, wout_ref[j])
    return x + 0.5 * acc_ref[...]


def _ffn_kernel(x_ref, g_ref, win_ref, wout_ref, o_ref, acc_ref):
    o_ref[...] = _ffn_half_step(x_ref[...], g_ref, win_ref, wout_ref, acc_ref)


def _post_mixer_kernel(x_ref, a_ref, wo_ref, g_ref, win_ref, wout_ref, p_ref, gp_ref, wg_ref, wp_ref, gf_ref,
                       o_ref, acc_ref, *, final):
    x = x_ref[...] + _dot(a_ref[...], wo_ref[...])
    x = _ffn_half_step(x, g_ref, win_ref, wout_ref, acc_ref)
    gate = jax.nn.sigmoid(_dot(_rms(x, gp_ref[...]).astype(BF16), wg_ref[...]))
    x = x + gate * _dot(p_ref[...].astype(BF16), wp_ref[...])
    if final:
        x = _rms(x, gf_ref[...])
    o_ref[...] = x


def _layer_spec(arr, *idx):
    n = len(idx)
    rest = arr.shape[n:]
    return pl.BlockSpec((None,) * n + rest, lambda i: idx + (0,) * len(rest), pipeline_mode=pl.Buffered(1))


def _ffn(x, w, li, wi, *, tm=512):
    T, D = x.shape
    tm = _row_tile(T, tm)
    tok = lambda n: pl.BlockSpec((tm, n), lambda i: (i, 0))
    return pl.pallas_call(
        _ffn_kernel,
        grid=(T // tm,),
        in_specs=[tok(D), _layer_spec(w["ffn_norm"], li, wi), _layer_spec(w["ffn_in"], li, wi),
                  _layer_spec(w["ffn_out"], li, wi)],
        out_specs=tok(D),
        out_shape=jax.ShapeDtypeStruct((T, D), F32),
        scratch_shapes=[pltpu.VMEM((tm, D), F32)],
        compiler_params=_params("arbitrary"),
        name="ffn",
    )(x, w["ffn_norm"], w["ffn_in"], w["ffn_out"])


def _post_mixer(x, a, w_o, p, w, li, mi, *, tm=512):
    T, D = x.shape
    tm = _row_tile(T, tm)
    tok = lambda n: pl.BlockSpec((tm, n), lambda i: (i, 0))
    return pl.pallas_call(
        functools.partial(_post_mixer_kernel, final=(li == DEPTH - 1)),
        grid=(T // tm,),
        in_specs=[tok(D), tok(a.shape[1]), _layer_spec(w_o, mi),
                  _layer_spec(w["ffn_norm"], li, 1), _layer_spec(w["ffn_in"], li, 1), _layer_spec(w["ffn_out"], li, 1),
                  pl.BlockSpec((None, tm, PLE_DIM), lambda i: (li, i, 0)),
                  _layer_spec(w["ple_norm"], li), _layer_spec(w["ple_gate"], li), _layer_spec(w["ple_proj"], li),
                  _layer_spec(w["final_norm"])],
        out_specs=tok(D),
        out_shape=jax.ShapeDtypeStruct((T, D), F32),
        scratch_shapes=[pltpu.VMEM((tm, D), F32)],
        compiler_params=_params("arbitrary"),
        name="post_mixer",
    )(x, a, w_o, w["ffn_norm"], w["ffn_in"], w["ffn_out"], p, w["ple_norm"], w["ple_gate"], w["ple_proj"],
      w["final_norm"])


def _log_sigmoid(x):
    return jnp.minimum(x, 0.0) - jnp.log(1.0 + jnp.exp(-jnp.abs(x)))


def _gla_proj_kernel(x_ref, g_ref, w_ref, wlo_ref, wup_ref, bup_ref,
                     q_ref, k_ref, v_ref, r_ref, gf_ref, gb_ref):
    xn = _rms(x_ref[...], g_ref[...]).astype(BF16)
    lo = _dot(xn, wlo_ref[...]).astype(BF16)
    W, V = GLA_QK_W, GLA_V_W

    def gate(cols):
        return _log_sigmoid(_dot(lo, wup_ref[:, cols]) + bup_ref[:, cols]) / GLA_GATE_TAU

    q_ref[...] = _dot(xn, w_ref[:, :W]) * (GLA_DK ** -0.5)
    gf_ref[...] = gate(slice(0, W))
    k_ref[...] = _dot(xn, w_ref[:, W:2 * W])
    v_ref[...] = _dot(xn, w_ref[:, 2 * W:2 * W + V]).astype(BF16)
    gb_ref[...] = gate(slice(W, 2 * W))
    r_ref[...] = _dot(xn, w_ref[:, 2 * W + V:])


def _gla_proj(x, g, w, wlo, wup, bup, *, tm=512):
    T, D = x.shape
    tm = _row_tile(T, tm)
    row = lambda n: pl.BlockSpec((tm, n), lambda i: (i, 0))
    return pl.pallas_call(
        _gla_proj_kernel,
        grid=(T // tm,),
        in_specs=[row(D), _const_spec((1, D)), _const_spec(w.shape), _const_spec(wlo.shape),
                  _const_spec(wup.shape), _const_spec(bup.shape)],
        out_specs=[row(GLA_QK_W), row(GLA_QK_W), row(GLA_V_W), row(GLA_V_W), row(GLA_QK_W), row(GLA_QK_W)],
        out_shape=[
            jax.ShapeDtypeStruct((T, GLA_QK_W), F32),
            jax.ShapeDtypeStruct((T, GLA_QK_W), F32),
            jax.ShapeDtypeStruct((T, GLA_V_W), BF16),
            jax.ShapeDtypeStruct((T, GLA_V_W), F32),
            jax.ShapeDtypeStruct((T, GLA_QK_W), F32),
            jax.ShapeDtypeStruct((T, GLA_QK_W), F32),
        ],
        compiler_params=_params("arbitrary"),
        name="gla_proj",
    )(x, g, w, wlo, wup, bup)


def _split2(g):
    hi = g.astype(BF16)
    lo = (g - hi.astype(F32)).astype(BF16)
    return hi, lo


def _gla_core_kernel(q_ref, k_ref, v_ref, gf_ref, gb_ref, r_ref, gn_ref, o_ref,
                     oacc_ref, sf_ref, sb_ref, *, L, T):
    C = GLA_CHUNK
    nc = T // C
    nsub = L // T
    DK = GLA_DK

    row = lax.broadcasted_iota(jnp.int32, (T, T), 0)
    col = lax.broadcasted_iota(jnp.int32, (T, T), 1)
    cbits = C.bit_length() - 1
    same = lax.shift_right_logical(row, cbits) == lax.shift_right_logical(col, cbits)
    mask_f = same & (col <= row)
    mask_b = same & (col > row)
    tri = jnp.where(mask_f, 1.0, 0.0).astype(BF16)

    def chunk_last(x):
        n = x.shape[1]
        x3 = x.reshape(nc, C, n)
        return jnp.broadcast_to(x3[:, C - 1:C, :], (nc, C, n)).reshape(T, n)

    def prefix(g):
        hi, lo = _split2(g)
        return _dot(tri, hi) + _dot(tri, lo)

    def decay_cols(tot_row):
        return jnp.transpose(jnp.broadcast_to(jnp.exp(tot_row), (DK, DK)))

    def apply_decay(s, dec):
        return jnp.concatenate([s[:, :DK] * dec, s[:, DK:] * dec], axis=1)

    def block_rows(t):
        return pl.ds(pl.multiple_of(t * T, T), T)

    def direction(rows, lg, tot, mask, s_ref, order):
        q = q_ref[0, rows, :]
        k = k_ref[0, rows, :]
        v = v_ref[0, rows, :]
        qd = (q * jnp.exp(lg)).astype(BF16)
        ki = (k * jnp.exp(-lg)).astype(BF16)
        ke = (k * jnp.exp(tot - lg)).astype(BF16)
        a = jnp.where(mask, _dot_nt(qd, ki), 0.0).astype(BF16)
        o = _dot(a, v)
        outs = [None] * nc
        for c in order:
            sl = slice(c * C, (c + 1) * C)
            s = s_ref[...]
            outs[c] = o[sl] + _dot(qd[sl], s.astype(BF16))
            dec = decay_cols(tot[c * C:c * C + 1, :])
            s_ref[...] = apply_decay(s, dec) + _dot_tn(ke[sl], v[sl])
        return jnp.concatenate(outs, axis=0)

    gn = gn_ref[...]

    def finish(rows, o):
        r = r_ref[0, rows, :]
        o_ref[0, rows, :] = (_rms(o, gn) * (r * jax.nn.sigmoid(r))).astype(BF16)

    def pair(i, first):
        rows_f = block_rows(i)
        rows_b = block_rows(nsub - 1 - i)
        gb = gb_ref[0, rows_b, :]
        p = prefix(jnp.concatenate([gf_ref[0, rows_f, :], gb], axis=1))
        tot = chunk_last(p)
        bf, totf = p[:, :DK], tot[:, :DK]
        pb, totb = p[:, DK:], tot[:, DK:]
        rb = totb - pb + gb
        o_f = direction(rows_f, bf, totf, mask_f, sf_ref, range(nc))
        o_b = direction(rows_b, rb, totb, mask_b, sb_ref, reversed(range(nc)))
        if first:
            oacc_ref[rows_f, :] = o_f
            oacc_ref[rows_b, :] = o_b
        else:
            finish(rows_f, oacc_ref[rows_f, :] + o_f)
            finish(rows_b, oacc_ref[rows_b, :] + o_b)

    sf_ref[...] = jnp.zeros_like(sf_ref)
    sb_ref[...] = jnp.zeros_like(sb_ref)
    half = nsub // 2
    unroll = next(u for u in (8, 4, 2, 1) if half % u == 0)
    lax.fori_loop(0, half, lambda i, c: (pair(i, True), c)[1], 0, unroll=unroll)
    lax.fori_loop(half, nsub, lambda i, c: (pair(i, False), c)[1], 0, unroll=unroll)


def _gla_core(q, k, v, gf, gb, r, gn):
    B, L, _ = q.shape
    T = min(GLA_SUB, L)
    assert L % (2 * T) == 0 and T % GLA_CHUNK == 0
    qk = pl.BlockSpec((1, L, GLA_DK), lambda b, h: (b, 0, h))
    vv = pl.BlockSpec((1, L, GLA_DV), lambda b, h: (b, 0, h))
    return pl.pallas_call(
        functools.partial(_gla_core_kernel, L=L, T=T),
        grid=(B, GLA_HEADS),
        in_specs=[qk, qk, vv, qk, qk, vv, _const_spec((1, GLA_DV))],
        out_specs=vv,
        out_shape=jax.ShapeDtypeStruct((B, L, GLA_V_W), BF16),
        scratch_shapes=[pltpu.VMEM((L, GLA_DV), F32),
                        pltpu.VMEM((GLA_DK, GLA_DV), F32),
                        pltpu.VMEM((GLA_DK, GLA_DV), F32)],
        compiler_params=_params("arbitrary", "arbitrary"),
        name="gla_core",
    )(q, k, v, gf, gb, r, gn)


def _mla_proj_kernel(x_ref, g_ref, win_ref, qn_ref, kvn_ref, wuq_ref, wukv_ref, cos_ref, sin_ref,
                     qt_ref, k_ref, vt_ref):
    H = MLA_HEADS
    xn = _rms(x_ref[...], g_ref[...]).astype(BF16)
    h = _dot(xn, win_ref[...])
    cq = h[:, :MLA_Q_RANK]
    ckv = h[:, MLA_Q_RANK:MLA_Q_RANK + MLA_KV_RANK]
    o = MLA_Q_RANK + MLA_KV_RANK
    cos = cos_ref[...]
    sin = sin_ref[...]
    kr = (h[:, o:o + LANES] * cos + h[:, o + LANES:o + 2 * LANES] * sin).astype(BF16)
    qa = _dot(_rms(cq, qn_ref[...]).astype(BF16), wuq_ref[...])
    kv = _dot(_rms(ckv, kvn_ref[...]).astype(BF16), wukv_ref[...])
    scale = (MLA_NOPE + MLA_ROPE) ** -0.5 * LOG2_E
    lane = lax.broadcasted_iota(jnp.int32, (1, LANES), 1)
    ro = H * MLA_NOPE
    rw = H * MLA_ROPE
    for hd in range(H):
        nope = qa[:, hd * LANES:(hd + 1) * LANES]
        g = hd // 2
        rp = qa[:, ro + g * LANES:ro + (g + 1) * LANES]
        rs = qa[:, ro + rw + g * LANES:ro + rw + (g + 1) * LANES]
        own = (lane < MLA_ROPE) if hd % 2 == 0 else (lane >= MLA_ROPE)
        rope = jnp.where(own, (rp * cos + rs * sin) * scale, 0.0)
        qt_ref[0, hd * MLA_QK_PAD:hd * MLA_QK_PAD + LANES, :] = jnp.transpose(nope * scale).astype(BF16)
        qt_ref[0, hd * MLA_QK_PAD + LANES:(hd + 1) * MLA_QK_PAD, :] = jnp.transpose(rope).astype(BF16)
        k_ref[:, hd * MLA_QK_PAD:hd * MLA_QK_PAD + LANES] = kv[:, hd * LANES:(hd + 1) * LANES].astype(BF16)
        k_ref[:, hd * MLA_QK_PAD + LANES:(hd + 1) * MLA_QK_PAD] = kr
    vt_ref[0] = jnp.transpose(kv[:, H * LANES:]).astype(BF16)


def _mla_proj(x, g, win, qn, kvn, wuq, wukv, cos, sin, *, S, tm=512):
    T, D = x.shape
    tm = _row_tile(S, tm)
    ns = S // tm
    row = lambda n: pl.BlockSpec((tm, n), lambda i: (i, 0))
    pos = pl.BlockSpec((tm, LANES), lambda i: (i % ns, 0))
    QW = MLA_HEADS * MLA_QK_PAD
    VW = MLA_HEADS * MLA_V
    return pl.pallas_call(
        _mla_proj_kernel,
        grid=(T // tm,),
        in_specs=[row(D), _const_spec((1, D)), _const_spec(win.shape), _const_spec(qn.shape),
                  _const_spec(kvn.shape), _const_spec(wuq.shape), _const_spec(wukv.shape), pos, pos],
        out_specs=[pl.BlockSpec((1, QW, tm), lambda i: (i // ns, 0, i % ns)), row(QW),
                   pl.BlockSpec((1, VW, tm), lambda i: (i // ns, 0, i % ns))],
        out_shape=[jax.ShapeDtypeStruct((T // S, QW, S), BF16),
                   jax.ShapeDtypeStruct((T, QW), BF16),
                   jax.ShapeDtypeStruct((T // S, VW, S), BF16)],
        compiler_params=_params("arbitrary"),
        name="mla_proj",
    )(x, g, win, qn, kvn, wuq, wukv, cos, sin)


def _mla_attn_kernel(qt_ref, k_ref, vt_ref, o_ref, s0_ref, s1_ref, m0_ref, m1_ref, *, nt):
    i = pl.program_id(0)
    s_refs = (s0_ref, s1_ref)
    m_refs = (m0_ref, m1_ref)

    S = k_ref.shape[1]
    kc = min(MLA_KEY_CHUNK, S)

    def step(score_slot, finish_slot):
        n = S // kc
        state = {}

        def score_chunk(j):
            rows = slice(j * kc, (j + 1) * kc)
            st = _dot(k_ref[0, rows, :], qt_ref[0])
            s_refs[score_slot][rows, :] = st
            mj = jnp.max(st, axis=0, keepdims=True)
            state["m"] = mj if j == 0 else jnp.maximum(state["m"], mj)
            if j == n - 1:
                m_refs[score_slot][...] = state["m"]

        def finish_chunk(j):
            rows = slice(j * kc, (j + 1) * kc)
            p = jnp.exp2(s_refs[finish_slot][rows, :] - m_refs[finish_slot][...])
            lj = jnp.sum(p, axis=0, keepdims=True)
            aj = _dot(vt_ref[0, :, rows], p.astype(BF16))
            state["l"] = lj if j == 0 else state["l"] + lj
            state["acc"] = aj if j == 0 else state["acc"] + aj
            if j == n - 1:
                o_ref[0] = jnp.transpose(state["acc"] / state["l"]).astype(BF16)

        if finish_slot is None:
            order = [("s", j) for j in range(n)]
        elif score_slot is None:
            order = [("f", j) for j in range(n)]
        else:
            order = [("s", 0), ("f", 0)]
            for j in range(1, n):
                order += [("f", j), ("s", j)]
        for kind, j in order:
            (score_chunk if kind == "s" else finish_chunk)(j)

    @pl.when(i == 0)
    def _():
        step(0, None)

    for par in range(2):
        @pl.when((i > 0) & (i < nt) & (i % 2 == par))
        def _():
            step(par, 1 - par)

    @pl.when(i == nt)
    def _():
        step(None, (nt - 1) % 2)


def _mla_attn(qt, k, vt, *, tq=1024):
    B, S, _ = k.shape
    tq = _row_tile(S, tq)
    nq = S // tq
    nt = B * MLA_HEADS * nq

    def tile(t):
        bh = t // nq
        return bh // MLA_HEADS, bh % MLA_HEADS, t % nq

    def scored(t):
        return tile(jnp.minimum(t, nt - 1))

    def finished(t):
        return tile(jnp.maximum(t - 1, 0))

    def q_map(t):
        b, h, i = scored(t)
        return b, h, i

    def k_map(t):
        b, h, _ = scored(t)
        return b, 0, h

    def v_map(t):
        b, h, _ = finished(t)
        return b, h, 0

    def o_map(t):
        b, h, i = finished(t)
        return b, i, h

    return pl.pallas_call(
        functools.partial(_mla_attn_kernel, nt=nt),
        grid=(nt + 1,),
        in_specs=[
            pl.BlockSpec((1, MLA_QK_PAD, tq), q_map),
            pl.BlockSpec((1, S, MLA_QK_PAD), k_map),
            pl.BlockSpec((1, MLA_V, S), v_map),
        ],
        out_specs=pl.BlockSpec((1, tq, MLA_V), o_map),
        out_shape=jax.ShapeDtypeStruct((B, S, MLA_HEADS * MLA_V), BF16),
        scratch_shapes=[pltpu.VMEM((S, tq), F32), pltpu.VMEM((S, tq), F32),
                        pltpu.VMEM((1, tq), F32), pltpu.VMEM((1, tq), F32)],
        compiler_params=_params("arbitrary"),
        name="mla_attn",
    )(qt, k, vt)


def _prep_weights(ffn_norm, ffn_w_in, ffn_w_out, mix_norm, ple_norm, ple_w_gate, ple_w_proj,
                  gla_w_in, gla_w_gf_up, gla_b_gf, gla_w_gb_up, gla_b_gb, gla_out_norm, gla_w_out,
                  mla_w_in, mla_q_norm, mla_kv_norm, mla_w_uq, mla_w_ukv, mla_w_out, final_norm):
    nck = D_FF // FFN_CHUNK
    w = {}
    w["ffn_in"] = ffn_w_in.astype(BF16)
    w["ffn_out"] = ffn_w_out.reshape(DEPTH, 2, nck, FFN_CHUNK, D_MODEL).astype(BF16)
    w["ffn_norm"] = ffn_norm.reshape(DEPTH, 2, 1, D_MODEL)
    w["mix_norm"] = mix_norm.reshape(DEPTH, 1, D_MODEL)
    w["ple_norm"] = ple_norm.reshape(DEPTH, 1, D_MODEL)
    w["ple_gate"] = ple_w_gate.astype(BF16)
    w["ple_proj"] = ple_w_proj.astype(BF16)
    w["final_norm"] = final_norm.reshape(1, D_MODEL)

    main = 2 * GLA_QK_W + 2 * GLA_V_W
    NG = gla_w_in.shape[0]
    R = GLA_GATE_RANK
    w["gla_in"] = gla_w_in[..., :main].astype(BF16)
    w["gla_lo"] = jnp.pad(gla_w_in[..., main:], ((0, 0), (0, 0), (0, LANES - 2 * R))).astype(BF16)
    upm = jnp.zeros((NG, LANES, 2 * GLA_QK_W), F32)
    upm = upm.at[:, :R, :GLA_QK_W].set(gla_w_gf_up).at[:, R:2 * R, GLA_QK_W:].set(gla_w_gb_up)
    w["gla_up"] = upm.astype(BF16)
    w["gla_bup"] = jnp.concatenate([gla_b_gf, gla_b_gb], axis=-1).reshape(NG, 1, 2 * GLA_QK_W)
    w["gla_out_norm"] = gla_out_norm.reshape(NG, 1, GLA_DV)
    w["gla_out"] = gla_w_out.astype(BF16)

    NM = mla_w_in.shape[0]
    H = MLA_HEADS
    half = MLA_ROPE // 2
    swap = lambda t: jnp.concatenate([t[..., half:], t[..., :half]], axis=-1)
    twice = lambda t: jnp.concatenate([t, t], axis=-1)
    o = MLA_Q_RANK + MLA_KV_RANK
    kr = mla_w_in[..., o:]
    w["mla_in"] = jnp.concatenate([mla_w_in[..., :o], twice(kr), twice(swap(kr))], axis=-1).astype(BF16)
    uq = mla_w_uq.reshape(NM, MLA_Q_RANK, H, MLA_NOPE + MLA_ROPE)
    nope = uq[..., :MLA_NOPE].reshape(NM, MLA_Q_RANK, H * MLA_NOPE)
    rp = uq[..., MLA_NOPE:]
    w["mla_uq"] = jnp.concatenate(
        [nope, rp.reshape(NM, MLA_Q_RANK, H * MLA_ROPE), swap(rp).reshape(NM, MLA_Q_RANK, H * MLA_ROPE)],
        axis=-1).astype(BF16)
    ukv = mla_w_ukv.reshape(NM, MLA_KV_RANK, H, MLA_NOPE + MLA_V)
    w["mla_ukv"] = jnp.concatenate(
        [ukv[..., :MLA_NOPE].reshape(NM, MLA_KV_RANK, H * MLA_NOPE),
         ukv[..., MLA_NOPE:].reshape(NM, MLA_KV_RANK, H * MLA_V)], axis=-1).astype(BF16)
    w["mla_q_norm"] = mla_q_norm.reshape(NM, 1, MLA_Q_RANK)
    w["mla_kv_norm"] = mla_kv_norm.reshape(NM, 1, MLA_KV_RANK)
    w["mla_out"] = mla_w_out.astype(BF16)
    return w


def _rope_tables(S):
    inv_freq = ROPE_THETA ** (-jnp.arange(0, MLA_ROPE, 2, dtype=F32) / MLA_ROPE)
    ang = jnp.arange(S, dtype=F32)[:, None] * inv_freq[None, :]
    cos, sin = jnp.cos(ang), jnp.sin(ang)
    return jnp.concatenate([cos, cos, cos, cos], axis=1), jnp.concatenate([-sin, sin, -sin, sin], axis=1)


def _trunk(x, p, w):
    B, S, D = x.shape
    T = B * S
    x = x.reshape(T, D)
    p = p.reshape(DEPTH, T, PLE_DIM)
    cos, sin = _rope_tables(S)
    for i in range(DEPTH):
        j = i // 2
        x = _ffn(x, w, i, 0)
        if i % 2 == 0:
            q, k, v, r, gf, gb = _gla_proj(x, w["mix_norm"][i], w["gla_in"][j], w["gla_lo"][j],
                                           w["gla_up"][j], w["gla_bup"][j])
            sh = lambda t: t.reshape(B, S, t.shape[-1])
            o = _gla_core(sh(q), sh(k), sh(v), sh(gf), sh(gb), sh(r), w["gla_out_norm"][j])
            x = _post_mixer(x, o.reshape(T, GLA_V_W), w["gla_out"], p, w, i, j)
        else:
            qt, k, vt = _mla_proj(x, w["mix_norm"][i], w["mla_in"][j], w["mla_q_norm"][j], w["mla_kv_norm"][j],
                                  w["mla_uq"][j], w["mla_ukv"][j], cos, sin, S=S)
            o = _mla_attn(qt, k.reshape(B, S, k.shape[-1]), vt)
            x = _post_mixer(x, o.reshape(T, MLA_HEADS * MLA_V), w["mla_out"], p, w, i, j)
    return x.reshape(B, S, D)


def kernel(x_prompt, x_sample, p_prompt, p_sample, ffn_norm, ffn_w_in, ffn_w_out, mix_norm, ple_norm,
           ple_w_gate, ple_w_proj, gla_w_in, gla_w_gf_up, gla_b_gf, gla_w_gb_up, gla_b_gb, gla_out_norm,
           gla_w_out, mla_w_in, mla_q_norm, mla_kv_norm, mla_w_uq, mla_w_ukv, mla_w_out, final_norm):
    w = _prep_weights(ffn_norm, ffn_w_in, ffn_w_out, mix_norm, ple_norm, ple_w_gate, ple_w_proj,
                      gla_w_in, gla_w_gf_up, gla_b_gf, gla_w_gb_up, gla_b_gb, gla_out_norm, gla_w_out,
                      mla_w_in, mla_q_norm, mla_kv_norm, mla_w_uq, mla_w_ukv, mla_w_out, final_norm)
    return (_trunk(x_prompt, p_prompt, w), _trunk(x_sample, p_sample, w))
```

```python
import functools

import jax
import jax.numpy as jnp
from jax import lax
from jax.experimental import pallas as pl
from jax.experimental.pallas import tpu as pltpu

F32 = jnp.float32
BF16 = jnp.bfloat16

D_MODEL = 1024
DEPTH = 2
PLE_DIM = 256
D_FF = 2816
NORM_EPS = 1e-6

GLA_HEADS = 4
GLA_DK = 128
GLA_DV = 256
GLA_GATE_RANK = 16
GLA_GATE_TAU = 16.0
GLA_CHUNK = 64
GLA_QK_W = GLA_HEADS * GLA_DK
GLA_V_W = GLA_HEADS * GLA_DV

MLA_HEADS = 8
MLA_Q_RANK = 384
MLA_KV_RANK = 256
MLA_NOPE = 128
MLA_ROPE = 64
MLA_V = 128
ROPE_THETA = 10000.0
LOG2_E = 1.4426950408889634

LANES = 128
MLA_QK_PAD = 2 * LANES
VMEM_LIMIT_BYTES = 56 * 1024 * 1024

FFN_CHUNK = 256
GLA_SUB = 256
MLA_KEY_CHUNK = 256


def _params(*sem):
    return pltpu.CompilerParams(dimension_semantics=sem, vmem_limit_bytes=VMEM_LIMIT_BYTES)


def _const_spec(shape):
    nd = len(shape)
    return pl.BlockSpec(shape, lambda *_: (0,) * nd)


def _row_tile(n, pref):
    t = min(pref, n)
    assert n % t == 0, (n, t)
    return t


def _rms(x, g):
    ms = jnp.mean(x * x, axis=-1, keepdims=True)
    return x * lax.rsqrt(ms + NORM_EPS) * g


def _dot(a, b):
    return jnp.dot(a, b, preferred_element_type=F32)


def _dot_nt(a, b):
    return lax.dot_general(a, b, (((1,), (1,)), ((), ())), preferred_element_type=F32)


def _dot_tn(a, b):
    return lax.dot_general(a, b, (((0,), (0,)), ((), ())), preferred_element_type=F32)


def _ffn_half_step(x, g_ref, win_ref, wout_ref, acc_ref):
    nck, ck, _ = wout_ref.shape
    d_ff = nck * ck
    xn = _rms(x, g_ref[...]).astype(BF16)
    acc_ref[...] = jnp.zeros_like(acc_ref)
    for j in range(nck):
        gate = _dot(xn, win_ref[:, j * ck:(j + 1) * ck])
        up = _dot(xn, win_ref[:, d_ff + j * ck:d_ff + (j + 1) * ck])
        a = (gate * jax.nn.sigmoid(gate) * up).astype(BF16)
        acc_ref[...] += _dot(a, wout_ref[j])
    return x + 0.5 * acc_ref[...]


def _ffn_kernel(x_ref, g_ref, win_ref, wout_ref, o_ref):
    o_ref[...] = _ffn_half_step(x_ref[...], g_ref, win_ref, wout_ref, o_ref)


def _post_mixer_kernel(x_ref, a_ref, wo_ref, g_ref, win_ref, wout_ref, p_ref, gp_ref, wg_ref, wp_ref, gf_ref,
                       o_ref, *, final):
    x = x_ref[...] + _dot(a_ref[...], wo_ref[...])
    x = _ffn_half_step(x, g_ref, win_ref, wout_ref, o_ref)
    gate = jax.nn.sigmoid(_dot(_rms(x, gp_ref[...]).astype(BF16), wg_ref[...]))
    x = x + gate * _dot(p_ref[...].astype(BF16), wp_ref[...])
    if final:
        x = _rms(x, gf_ref[...])
    o_ref[...] = x


def _layer_spec(arr, *idx):
    n = len(idx)
    rest = arr.shape[n:]
    return pl.BlockSpec((None,) * n + rest, lambda i: idx + (0,) * len(rest), pipeline_mode=pl.Buffered(1))


def _ffn(x, w, li, wi, *, tm=1024):
    T, D = x.shape
    tm = _row_tile(T, tm)
    tok = lambda n: pl.BlockSpec((tm, n), lambda i: (i, 0))
    return pl.pallas_call(
        _ffn_kernel,
        grid=(T // tm,),
        in_specs=[tok(D), _layer_spec(w["ffn_norm"], li, wi), _layer_spec(w["ffn_in"], li, wi),
                  _layer_spec(w["ffn_out"], li, wi)],
        out_specs=tok(D),
        out_shape=jax.ShapeDtypeStruct((T, D), F32),
        compiler_params=_params("arbitrary"),
        name="ffn",
    )(x, w["ffn_norm"], w["ffn_in"], w["ffn_out"])


def _post_mixer(x, a, w_o, p, w, li, mi, *, tm=1024):
    T, D = x.shape
    tm = _row_tile(T, tm)
    tok = lambda n: pl.BlockSpec((tm, n), lambda i: (i, 0))
    return pl.pallas_call(
        functools.partial(_post_mixer_kernel, final=(li == DEPTH - 1)),
        grid=(T // tm,),
        in_specs=[tok(D), tok(a.shape[1]), _layer_spec(w_o, mi),
                  _layer_spec(w["ffn_norm"], li, 1), _layer_spec(w["ffn_in"], li, 1), _layer_spec(w["ffn_out"], li, 1),
                  pl.BlockSpec((None, tm, PLE_DIM), lambda i: (li, i, 0)),
                  _layer_spec(w["ple_norm"], li), _layer_spec(w["ple_gate"], li), _layer_spec(w["ple_proj"], li),
                  _layer_spec(w["final_norm"])],
        out_specs=tok(D),
        out_shape=jax.ShapeDtypeStruct((T, D), F32),
        compiler_params=_params("arbitrary"),
        name="post_mixer",
    )(x, a, w_o, w["ffn_norm"], w["ffn_in"], w["ffn_out"], p, w["ple_norm"], w["ple_gate"], w["ple_proj"],
      w["final_norm"])


def _log_sigmoid(x):
    return jnp.minimum(x, 0.0) - jnp.log(1.0 + jnp.exp(-jnp.abs(x)))


def _gla_proj_kernel(x_ref, g_ref, w_ref, wlo_ref, wup_ref, bup_ref,
                     q_ref, k_ref, v_ref, r_ref, gf_ref, gb_ref):
    xn = _rms(x_ref[...], g_ref[...]).astype(BF16)
    lo = _dot(xn, wlo_ref[...]).astype(BF16)
    W, V = GLA_QK_W, GLA_V_W

    def gate(cols):
        return _log_sigmoid(_dot(lo, wup_ref[:, cols]) + bup_ref[:, cols]) / GLA_GATE_TAU

    q_ref[...] = _dot(xn, w_ref[:, :W]) * (GLA_DK ** -0.5)
    gf_ref[...] = gate(slice(0, W))
    k_ref[...] = _dot(xn, w_ref[:, W:2 * W])
    v_ref[...] = _dot(xn, w_ref[:, 2 * W:2 * W + V]).astype(BF16)
    gb_ref[...] = gate(slice(W, 2 * W))
    r_ref[...] = _dot(xn, w_ref[:, 2 * W + V:])


def _gla_proj(x, g, w, wlo, wup, bup, *, tm=512):
    T, D = x.shape
    tm = _row_tile(T, tm)
    row = lambda n: pl.BlockSpec((tm, n), lambda i: (i, 0))
    return pl.pallas_call(
        _gla_proj_kernel,
        grid=(T // tm,),
        in_specs=[row(D), _const_spec((1, D)), _const_spec(w.shape), _const_spec(wlo.shape),
                  _const_spec(wup.shape), _const_spec(bup.shape)],
        out_specs=[row(GLA_QK_W), row(GLA_QK_W), row(GLA_V_W), row(GLA_V_W), row(GLA_QK_W), row(GLA_QK_W)],
        out_shape=[
            jax.ShapeDtypeStruct((T, GLA_QK_W), F32),
            jax.ShapeDtypeStruct((T, GLA_QK_W), F32),
            jax.ShapeDtypeStruct((T, GLA_V_W), BF16),
            jax.ShapeDtypeStruct((T, GLA_V_W), F32),
            jax.ShapeDtypeStruct((T, GLA_QK_W), F32),
            jax.ShapeDtypeStruct((T, GLA_QK_W), F32),
        ],
        compiler_params=_params("arbitrary"),
        name="gla_proj",
    )(x, g, w, wlo, wup, bup)


def _split2(g):
    hi = g.astype(BF16)
    lo = (g - hi.astype(F32)).astype(BF16)
    return hi, lo


def _gla_core_kernel(q_ref, k_ref, v_ref, gf_ref, gb_ref, r_ref, gn_ref, o_ref,
                     oacc_ref, sf_ref, sb_ref, *, L, T):
    C = GLA_CHUNK
    nc = T // C
    nsub = L // T
    DK = GLA_DK

    row = lax.broadcasted_iota(jnp.int32, (T, T), 0)
    col = lax.broadcasted_iota(jnp.int32, (T, T), 1)
    cbits = C.bit_length() - 1
    same = lax.shift_right_logical(row, cbits) == lax.shift_right_logical(col, cbits)
    mask_f = same & (col <= row)
    mask_b = same & (col > row)
    tri = jnp.where(mask_f, 1.0, 0.0).astype(BF16)

    def chunk_last(x):
        n = x.shape[1]
        x3 = x.reshape(nc, C, n)
        return jnp.broadcast_to(x3[:, C - 1:C, :], (nc, C, n)).reshape(T, n)

    def prefix(g):
        hi, lo = _split2(g)
        return _dot(tri, hi) + _dot(tri, lo)

    def decay_cols(tot_row):
        return jnp.transpose(jnp.broadcast_to(jnp.exp(tot_row), (DK, DK)))

    def apply_decay(s, dec):
        return jnp.concatenate([s[:, :DK] * dec, s[:, DK:] * dec], axis=1)

    def block_rows(t):
        return pl.ds(pl.multiple_of(t * T, T), T)

    def direction(rows, lg, tot, mask, s_ref, order):
        q = q_ref[0, rows, :]
        k = k_ref[0, rows, :]
        v = v_ref[0, rows, :]
        qd = (q * jnp.exp(lg)).astype(BF16)
        ki = (k * jnp.exp(-lg)).astype(BF16)
        ke = (k * jnp.exp(tot - lg)).astype(BF16)
        a = jnp.where(mask, _dot_nt(qd, ki), 0.0).astype(BF16)
        o = _dot(a, v)
        outs = [None] * nc
        for c in order:
            sl = slice(c * C, (c + 1) * C)
            s = s_ref[...]
            outs[c] = o[sl] + _dot(qd[sl], s.astype(BF16))
            dec = decay_cols(tot[c * C:c * C + 1, :])
            s_ref[...] = apply_decay(s, dec) + _dot_tn(ke[sl], v[sl])
        return jnp.concatenate(outs, axis=0)

    gn = gn_ref[...]

    def finish(rows, o):
        r = r_ref[0, rows, :]
        o_ref[0, rows, :] = (_rms(o, gn) * (r * jax.nn.sigmoid(r))).astype(BF16)

    def pair(i, first):
        rows_f = block_rows(i)
        rows_b = block_rows(nsub - 1 - i)
        gb = gb_ref[0, rows_b, :]
        p = prefix(jnp.concatenate([gf_ref[0, rows_f, :], gb], axis=1))
        tot = chunk_last(p)
        bf, totf = p[:, :DK], tot[:, :DK]
        pb, totb = p[:, DK:], tot[:, DK:]
        rb = totb - pb + gb
        o_f = direction(rows_f, bf, totf, mask_f, sf_ref, range(nc))
        o_b = direction(rows_b, rb, totb, mask_b, sb_ref, reversed(range(nc)))
        if first:
            oacc_ref[rows_f, :] = o_f
            oacc_ref[rows_b, :] = o_b
        else:
            finish(rows_f, oacc_ref[rows_f, :] + o_f)
            finish(rows_b, oacc_ref[rows_b, :] + o_b)

    sf_ref[...] = jnp.zeros_like(sf_ref)
    sb_ref[...] = jnp.zeros_like(sb_ref)
    half = nsub // 2
    unroll = next(u for u in (8, 4, 2, 1) if half % u == 0)
    lax.fori_loop(0, half, lambda i, c: (pair(i, True), c)[1], 0, unroll=unroll)
    lax.fori_loop(half, nsub, lambda i, c: (pair(i, False), c)[1], 0, unroll=unroll)


def _gla_core(q, k, v, gf, gb, r, gn):
    B, L, _ = q.shape
    T = min(GLA_SUB, L)
    assert L % (2 * T) == 0 and T % GLA_CHUNK == 0
    qk = pl.BlockSpec((1, L, GLA_DK), lambda b, h: (b, 0, h))
    vv = pl.BlockSpec((1, L, GLA_DV), lambda b, h: (b, 0, h))
    return pl.pallas_call(
        functools.partial(_gla_core_kernel, L=L, T=T),
        grid=(B, GLA_HEADS),
        in_specs=[qk, qk, vv, qk, qk, vv, _const_spec((1, GLA_DV))],
        out_specs=vv,
        out_shape=jax.ShapeDtypeStruct((B, L, GLA_V_W), BF16),
        scratch_shapes=[pltpu.VMEM((L, GLA_DV), F32),
                        pltpu.VMEM((GLA_DK, GLA_DV), F32),
                        pltpu.VMEM((GLA_DK, GLA_DV), F32)],
        compiler_params=_params("arbitrary", "arbitrary"),
        name="gla_core",
    )(q, k, v, gf, gb, r, gn)


def _mla_proj_kernel(x_ref, g_ref, win_ref, qn_ref, kvn_ref, wuq_ref, wukv_ref, cos_ref, sin_ref,
                     qt_ref, k_ref, vt_ref):
    H = MLA_HEADS
    xn = _rms(x_ref[...], g_ref[...]).astype(BF16)
    h = _dot(xn, win_ref[...])
    cq = h[:, :MLA_Q_RANK]
    ckv = h[:, MLA_Q_RANK:MLA_Q_RANK + MLA_KV_RANK]
    o = MLA_Q_RANK + MLA_KV_RANK
    cos = cos_ref[...]
    sin = sin_ref[...]
    kr = (h[:, o:o + LANES] * cos + h[:, o + LANES:o + 2 * LANES] * sin).astype(BF16)
    qa = _dot(_rms(cq, qn_ref[...]).astype(BF16), wuq_ref[...])
    kv = _dot(_rms(ckv, kvn_ref[...]).astype(BF16), wukv_ref[...])
    scale = (MLA_NOPE + MLA_ROPE) ** -0.5 * LOG2_E
    lane = lax.broadcasted_iota(jnp.int32, (1, LANES), 1)
    ro = H * MLA_NOPE
    rw = H * MLA_ROPE
    for hd in range(H):
        nope = qa[:, hd * LANES:(hd + 1) * LANES]
        g = hd // 2
        rp = qa[:, ro + g * LANES:ro + (g + 1) * LANES]
        rs = qa[:, ro + rw + g * LANES:ro + rw + (g + 1) * LANES]
        own = (lane < MLA_ROPE) if hd % 2 == 0 else (lane >= MLA_ROPE)
        rope = jnp.where(own, (rp * cos + rs * sin) * scale, 0.0)
        qt_ref[0, hd * MLA_QK_PAD:hd * MLA_QK_PAD + LANES, :] = jnp.transpose(nope * scale).astype(BF16)
        qt_ref[0, hd * MLA_QK_PAD + LANES:(hd + 1) * MLA_QK_PAD, :] = jnp.transpose(rope).astype(BF16)
        k_ref[:, hd * MLA_QK_PAD:hd * MLA_QK_PAD + LANES] = kv[:, hd * LANES:(hd + 1) * LANES].astype(BF16)
        k_ref[:, hd * MLA_QK_PAD + LANES:(hd + 1) * MLA_QK_PAD] = kr
    vt_ref[0] = jnp.transpose(kv[:, H * LANES:]).astype(BF16)


def _mla_proj(x, g, win, qn, kvn, wuq, wukv, cos, sin, *, S, tm=512):
    T, D = x.shape
    tm = _row_tile(S, tm)
    ns = S // tm
    row = lambda n: pl.BlockSpec((tm, n), lambda i: (i, 0))
    pos = pl.BlockSpec((tm, LANES), lambda i: (i % ns, 0))
    QW = MLA_HEADS * MLA_QK_PAD
    VW = MLA_HEADS * MLA_V
    return pl.pallas_call(
        _mla_proj_kernel,
        grid=(T // tm,),
        in_specs=[row(D), _const_spec((1, D)), _const_spec(win.shape), _const_spec(qn.shape),
                  _const_spec(kvn.shape), _const_spec(wuq.shape), _const_spec(wukv.shape), pos, pos],
        out_specs=[pl.BlockSpec((1, QW, tm), lambda i: (i // ns, 0, i % ns)), row(QW),
                   pl.BlockSpec((1, VW, tm), lambda i: (i // ns, 0, i % ns))],
        out_shape=[jax.ShapeDtypeStruct((T // S, QW, S), BF16),
                   jax.ShapeDtypeStruct((T, QW), BF16),
                   jax.ShapeDtypeStruct((T // S, VW, S), BF16)],
        compiler_params=_params("arbitrary"),
        name="mla_proj",
    )(x, g, win, qn, kvn, wuq, wukv, cos, sin)


def _mla_attn_kernel(qt_ref, k_ref, vt_ref, o_ref, s0_ref, s1_ref, m0_ref, m1_ref, *, nt):
    i = pl.program_id(0)
    s_refs = (s0_ref, s1_ref)
    m_refs = (m0_ref, m1_ref)

    S = k_ref.shape[1]
    kc = min(MLA_KEY_CHUNK, S)

    def step(score_slot, finish_slot):
        n = S // kc
        state = {}

        def score_chunk(j):
            rows = slice(j * kc, (j + 1) * kc)
            st = _dot(k_ref[0, rows, :], qt_ref[0])
            s_refs[score_slot][rows, :] = st
            mj = jnp.max(st, axis=0, keepdims=True)
            state["m"] = mj if j == 0 else jnp.maximum(state["m"], mj)
            if j == n - 1:
                m_refs[score_slot][...] = state["m"]

        def finish_chunk(j):
            rows = slice(j * kc, (j + 1) * kc)
            p = jnp.exp2(s_refs[finish_slot][rows, :] - m_refs[finish_slot][...])
            lj = jnp.sum(p, axis=0, keepdims=True)
            aj = _dot(vt_ref[0, :, rows], p.astype(BF16))
            state["l"] = lj if j == 0 else state["l"] + lj
            state["acc"] = aj if j == 0 else state["acc"] + aj
            if j == n - 1:
                o_ref[0] = jnp.transpose(state["acc"] / state["l"]).astype(BF16)

        if finish_slot is None:
            order = [("s", j) for j in range(n)]
        elif score_slot is None:
            order = [("f", j) for j in range(n)]
        else:
            order = [("s", 0), ("f", 0)]
            for j in range(1, n):
                order += [("f", j), ("s", j)]
        for kind, j in order:
            (score_chunk if kind == "s" else finish_chunk)(j)

    @pl.when(i == 0)
    def _():
        step(0, None)

    for par in range(2):
        @pl.when((i > 0) & (i < nt) & (i % 2 == par))
        def _():
            step(par, 1 - par)

    @pl.when(i == nt)
    def _():
        step(None, (nt - 1) % 2)


def _mla_attn(qt, k, vt, *, tq=1024):
    B, S, _ = k.shape
    tq = _row_tile(S, tq)
    nq = S // tq
    nt = B * MLA_HEADS * nq

    def tile(t):
        bh = t // nq
        return bh // MLA_HEADS, bh % MLA_HEADS, t % nq

    def scored(t):
        return tile(jnp.minimum(t, nt - 1))

    def finished(t):
        return tile(jnp.maximum(t - 1, 0))

    def q_map(t):
        b, h, i = scored(t)
        return b, h, i

    def k_map(t):
        b, h, _ = scored(t)
        return b, 0, h

    def v_map(t):
        b, h, _ = finished(t)
        return b, h, 0

    def o_map(t):
        b, h, i = finished(t)
        return b, i, h

    return pl.pallas_call(
        functools.partial(_mla_attn_kernel, nt=nt),
        grid=(nt + 1,),
        in_specs=[
            pl.BlockSpec((1, MLA_QK_PAD, tq), q_map),
            pl.BlockSpec((1, S, MLA_QK_PAD), k_map),
            pl.BlockSpec((1, MLA_V, S), v_map),
        ],
        out_specs=pl.BlockSpec((1, tq, MLA_V), o_map),
        out_shape=jax.ShapeDtypeStruct((B, S, MLA_HEADS * MLA_V), BF16),
        scratch_shapes=[pltpu.VMEM((S, tq), F32), pltpu.VMEM((S, tq), F32),
                        pltpu.VMEM((1, tq), F32), pltpu.VMEM((1, tq), F32)],
        compiler_params=_params("arbitrary"),
        name="mla_attn",
    )(qt, k, vt)


def _prep_weights(ffn_norm, ffn_w_in, ffn_w_out, mix_norm, ple_norm, ple_w_gate, ple_w_proj,
                  gla_w_in, gla_w_gf_up, gla_b_gf, gla_w_gb_up, gla_b_gb, gla_out_norm, gla_w_out,
                  mla_w_in, mla_q_norm, mla_kv_norm, mla_w_uq, mla_w_ukv, mla_w_out, final_norm):
    nck = D_FF // FFN_CHUNK
    w = {}
    w["ffn_in"] = ffn_w_in.astype(BF16)
    w["ffn_out"] = ffn_w_out.reshape(DEPTH, 2, nck, FFN_CHUNK, D_MODEL).astype(BF16)
    w["ffn_norm"] = ffn_norm.reshape(DEPTH, 2, 1, D_MODEL)
    w["mix_norm"] = mix_norm.reshape(DEPTH, 1, D_MODEL)
    w["ple_norm"] = ple_norm.reshape(DEPTH, 1, D_MODEL)
    w["ple_gate"] = ple_w_gate.astype(BF16)
    w["ple_proj"] = ple_w_proj.astype(BF16)
    w["final_norm"] = final_norm.reshape(1, D_MODEL)

    main = 2 * GLA_QK_W + 2 * GLA_V_W
    NG = gla_w_in.shape[0]
    R = GLA_GATE_RANK
    w["gla_in"] = gla_w_in[..., :main].astype(BF16)
    w["gla_lo"] = jnp.pad(gla_w_in[..., main:], ((0, 0), (0, 0), (0, LANES - 2 * R))).astype(BF16)
    upm = jnp.zeros((NG, LANES, 2 * GLA_QK_W), F32)
    upm = upm.at[:, :R, :GLA_QK_W].set(gla_w_gf_up).at[:, R:2 * R, GLA_QK_W:].set(gla_w_gb_up)
    w["gla_up"] = upm.astype(BF16)
    w["gla_bup"] = jnp.concatenate([gla_b_gf, gla_b_gb], axis=-1).reshape(NG, 1, 2 * GLA_QK_W)
    w["gla_out_norm"] = gla_out_norm.reshape(NG, 1, GLA_DV)
    w["gla_out"] = gla_w_out.astype(BF16)

    NM = mla_w_in.shape[0]
    H = MLA_HEADS
    half = MLA_ROPE // 2
    swap = lambda t: jnp.concatenate([t[..., half:], t[..., :half]], axis=-1)
    twice = lambda t: jnp.concatenate([t, t], axis=-1)
    o = MLA_Q_RANK + MLA_KV_RANK
    kr = mla_w_in[..., o:]
    w["mla_in"] = jnp.concatenate([mla_w_in[..., :o], twice(kr), twice(swap(kr))], axis=-1).astype(BF16)
    uq = mla_w_uq.reshape(NM, MLA_Q_RANK, H, MLA_NOPE + MLA_ROPE)
    nope = uq[..., :MLA_NOPE].reshape(NM, MLA_Q_RANK, H * MLA_NOPE)
    rp = uq[..., MLA_NOPE:]
    w["mla_uq"] = jnp.concatenate(
        [nope, rp.reshape(NM, MLA_Q_RANK, H * MLA_ROPE), swap(rp).reshape(NM, MLA_Q_RANK, H * MLA_ROPE)],
        axis=-1).astype(BF16)
    ukv = mla_w_ukv.reshape(NM, MLA_KV_RANK, H, MLA_NOPE + MLA_V)
    w["mla_ukv"] = jnp.concatenate(
        [ukv[..., :MLA_NOPE].reshape(NM, MLA_KV_RANK, H * MLA_NOPE),
         ukv[..., MLA_NOPE:].reshape(NM, MLA_KV_RANK, H * MLA_V)], axis=-1).astype(BF16)
    w["mla_q_norm"] = mla_q_norm.reshape(NM, 1, MLA_Q_RANK)
    w["mla_kv_norm"] = mla_kv_norm.reshape(NM, 1, MLA_KV_RANK)
    w["mla_out"] = mla_w_out.astype(BF16)
    return w


def _rope_tables(S):
    inv_freq = ROPE_THETA ** (-jnp.arange(0, MLA_ROPE, 2, dtype=F32) / MLA_ROPE)
    ang = jnp.arange(S, dtype=F32)[:, None] * inv_freq[None, :]
    cos, sin = jnp.cos(ang), jnp.sin(ang)
    return jnp.concatenate([cos, cos, cos, cos], axis=1), jnp.concatenate([-sin, sin, -sin, sin], axis=1)


def _trunk(x, p, w):
    B, S, D = x.shape
    T = B * S
    x = x.reshape(T, D)
    p = p.reshape(DEPTH, T, PLE_DIM)
    cos, sin = _rope_tables(S)
    for i in range(DEPTH):
        j = i // 2
        x = _ffn(x, w, i, 0)
        if i % 2 == 0:
            q, k, v, r, gf, gb = _gla_proj(x, w["mix_norm"][i], w["gla_in"][j], w["gla_lo"][j],
                                           w["gla_up"][j], w["gla_bup"][j])
            sh = lambda t: t.reshape(B, S, t.shape[-1])
            o = _gla_core(sh(q), sh(k), sh(v), sh(gf), sh(gb), sh(r), w["gla_out_norm"][j])
            x = _post_mixer(x, o.reshape(T, GLA_V_W), w["gla_out"], p, w, i, j)
        else:
            qt, k, vt = _mla_proj(x, w["mix_norm"][i], w["mla_in"][j], w["mla_q_norm"][j], w["mla_kv_norm"][j],
                                  w["mla_uq"][j], w["mla_ukv"][j], cos, sin, S=S)
            o = _mla_attn(qt, k.reshape(B, S, k.shape[-1]), vt)
            x = _post_mixer(x, o.reshape(T, MLA_HEADS * MLA_V), w["mla_out"], p, w, i, j)
    return x.reshape(B, S, D)


def kernel(x_prompt, x_sample, p_prompt, p_sample, ffn_norm, ffn_w_in, ffn_w_out, mix_norm, ple_norm,
           ple_w_gate, ple_w_proj, gla_w_in, gla_w_gf_up, gla_b_gf, gla_w_gb_up, gla_b_gb, gla_out_norm,
           gla_w_out, mla_w_in, mla_q_norm, mla_kv_norm, mla_w_uq, mla_w_ukv, mla_w_out, final_norm):
    w = _prep_weights(ffn_norm, ffn_w_in, ffn_w_out, mix_norm, ple_norm, ple_w_gate, ple_w_proj,
                      gla_w_in, gla_w_gf_up, gla_b_gf, gla_w_gb_up, gla_b_gb, gla_out_norm, gla_w_out,
                      mla_w_in, mla_q_norm, mla_kv_norm, mla_w_uq, mla_w_ukv, mla_w_out, final_norm)
    return (_trunk(x_prompt, p_prompt, w), _trunk(x_sample, p_sample, w))
```

```python
import functools

import jax
import jax.numpy as jnp
from jax import lax
from jax.experimental import pallas as pl
from jax.experimental.pallas import tpu as pltpu

F32 = jnp.float32
BF16 = jnp.bfloat16

D_MODEL = 1024
DEPTH = 2
PLE_DIM = 256
D_FF = 2816
NORM_EPS = 1e-6

GLA_HEADS = 4
GLA_DK = 128
GLA_DV = 256
GLA_GATE_RANK = 16
GLA_GATE_TAU = 16.0
GLA_CHUNK = 64
GLA_QK_W = GLA_HEADS * GLA_DK
GLA_V_W = GLA_HEADS * GLA_DV

MLA_HEADS = 8
MLA_Q_RANK = 384
MLA_KV_RANK = 256
MLA_NOPE = 128
MLA_ROPE = 64
MLA_V = 128
ROPE_THETA = 10000.0
LOG2_E = 1.4426950408889634

LANES = 128
MXU_TILE = 256
VMEM_LIMIT_BYTES = 56 * 1024 * 1024

MLA_QK_PAD = MXU_TILE
FFN_CHUNK = MXU_TILE
GLA_SUB = MXU_TILE
MLA_KEY_CHUNK = MXU_TILE
TOKEN_TILE = 1024
MLA_SCORE_BYTES = 32 * 1024 * 1024


def _params(*sem):
    return pltpu.CompilerParams(dimension_semantics=sem, vmem_limit_bytes=VMEM_LIMIT_BYTES)


def _const_spec(shape):
    nd = len(shape)
    return pl.BlockSpec(shape, lambda *_: (0,) * nd)


def _row_tile(n, pref):
    t = min(pref, n)
    assert n % t == 0, (n, t)
    return t


def _rms(x, g):
    ms = jnp.mean(x * x, axis=-1, keepdims=True)
    return x * lax.rsqrt(ms + NORM_EPS) * g


def _dot(a, b):
    return jnp.dot(a, b, preferred_element_type=F32)


def _dot_nt(a, b):
    return lax.dot_general(a, b, (((1,), (1,)), ((), ())), preferred_element_type=F32)


def _dot_tn(a, b):
    return lax.dot_general(a, b, (((0,), (0,)), ((), ())), preferred_element_type=F32)


def _ffn_half_step(x, g_ref, win_ref, wout_ref, acc_ref):
    nck, ck, _ = wout_ref.shape
    d_ff = nck * ck
    xn = _rms(x, g_ref[...]).astype(BF16)
    acc_ref[...] = jnp.zeros_like(acc_ref)
    for j in range(nck):
        gate = _dot(xn, win_ref[:, j * ck:(j + 1) * ck])
        up = _dot(xn, win_ref[:, d_ff + j * ck:d_ff + (j + 1) * ck])
        a = (gate * jax.nn.sigmoid(gate) * up).astype(BF16)
        acc_ref[...] += _dot(a, wout_ref[j])
    return x + 0.5 * acc_ref[...]


def _ffn_kernel(x_ref, g_ref, win_ref, wout_ref, o_ref):
    o_ref[...] = _ffn_half_step(x_ref[...], g_ref, win_ref, wout_ref, o_ref)


def _post_mixer_kernel(x_ref, a_ref, wo_ref, g_ref, win_ref, wout_ref, p_ref, gp_ref, wg_ref, wp_ref, gf_ref,
                       o_ref, *, final):
    x = x_ref[...] + _dot(a_ref[...], wo_ref[...])
    x = _ffn_half_step(x, g_ref, win_ref, wout_ref, o_ref)
    gate = jax.nn.sigmoid(_dot(_rms(x, gp_ref[...]).astype(BF16), wg_ref[...]))
    x = x + gate * _dot(p_ref[...].astype(BF16), wp_ref[...])
    if final:
        x = _rms(x, gf_ref[...])
    o_ref[...] = x


def _layer_spec(arr, *idx):
    n = len(idx)
    rest = arr.shape[n:]
    return pl.BlockSpec((None,) * n + rest, lambda i: idx + (0,) * len(rest), pipeline_mode=pl.Buffered(1))


def _ffn(x, w, li, wi):
    T, D = x.shape
    tm = _row_tile(T, TOKEN_TILE)
    tok = lambda n: pl.BlockSpec((tm, n), lambda i: (i, 0))
    return pl.pallas_call(
        _ffn_kernel,
        grid=(T // tm,),
        in_specs=[tok(D), _layer_spec(w["ffn_norm"], li, wi), _layer_spec(w["ffn_in"], li, wi),
                  _layer_spec(w["ffn_out"], li, wi)],
        out_specs=tok(D),
        out_shape=jax.ShapeDtypeStruct((T, D), F32),
        compiler_params=_params("arbitrary"),
        name="ffn",
    )(x, w["ffn_norm"], w["ffn_in"], w["ffn_out"])


def _post_mixer(x, a, w_o, p, w, li, mi):
    T, D = x.shape
    tm = _row_tile(T, TOKEN_TILE)
    tok = lambda n: pl.BlockSpec((tm, n), lambda i: (i, 0))
    return pl.pallas_call(
        functools.partial(_post_mixer_kernel, final=(li == DEPTH - 1)),
        grid=(T // tm,),
        in_specs=[tok(D), tok(a.shape[1]), _layer_spec(w_o, mi),
                  _layer_spec(w["ffn_norm"], li, 1), _layer_spec(w["ffn_in"], li, 1), _layer_spec(w["ffn_out"], li, 1),
                  pl.BlockSpec((None, tm, PLE_DIM), lambda i: (li, i, 0)),
                  _layer_spec(w["ple_norm"], li), _layer_spec(w["ple_gate"], li), _layer_spec(w["ple_proj"], li),
                  _layer_spec(w["final_norm"])],
        out_specs=tok(D),
        out_shape=jax.ShapeDtypeStruct((T, D), F32),
        compiler_params=_params("arbitrary"),
        name="post_mixer",
    )(x, a, w_o, w["ffn_norm"], w["ffn_in"], w["ffn_out"], p, w["ple_norm"], w["ple_gate"], w["ple_proj"],
      w["final_norm"])


def _log_sigmoid(x):
    return jnp.minimum(x, 0.0) - jnp.log(1.0 + jnp.exp(-jnp.abs(x)))


def _gla_proj_kernel(x_ref, g_ref, w_ref, wlo_ref, wup_ref, bup_ref,
                     q_ref, k_ref, v_ref, r_ref, gf_ref, gb_ref):
    xn = _rms(x_ref[...], g_ref[...]).astype(BF16)
    lo = _dot(xn, wlo_ref[...]).astype(BF16)
    W, V = GLA_QK_W, GLA_V_W

    def gate(cols):
        return _log_sigmoid(_dot(lo, wup_ref[:, cols]) + bup_ref[:, cols]) / GLA_GATE_TAU

    q_ref[...] = _dot(xn, w_ref[:, :W]) * (GLA_DK ** -0.5)
    gf_ref[...] = gate(slice(0, W))
    k_ref[...] = _dot(xn, w_ref[:, W:2 * W])
    v_ref[...] = _dot(xn, w_ref[:, 2 * W:2 * W + V]).astype(BF16)
    gb_ref[...] = gate(slice(W, 2 * W))
    r_ref[...] = _dot(xn, w_ref[:, 2 * W + V:])


def _gla_proj(x, g, w, wlo, wup, bup):
    T, D = x.shape
    tm = _row_tile(T, TOKEN_TILE)
    row = lambda n: pl.BlockSpec((tm, n), lambda i: (i, 0))
    return pl.pallas_call(
        _gla_proj_kernel,
        grid=(T // tm,),
        in_specs=[row(D), _const_spec((1, D)), _const_spec(w.shape), _const_spec(wlo.shape),
                  _const_spec(wup.shape), _const_spec(bup.shape)],
        out_specs=[row(GLA_QK_W), row(GLA_QK_W), row(GLA_V_W), row(GLA_V_W), row(GLA_QK_W), row(GLA_QK_W)],
        out_shape=[
            jax.ShapeDtypeStruct((T, GLA_QK_W), F32),
            jax.ShapeDtypeStruct((T, GLA_QK_W), F32),
            jax.ShapeDtypeStruct((T, GLA_V_W), BF16),
            jax.ShapeDtypeStruct((T, GLA_V_W), F32),
            jax.ShapeDtypeStruct((T, GLA_QK_W), F32),
            jax.ShapeDtypeStruct((T, GLA_QK_W), F32),
        ],
        compiler_params=_params("arbitrary"),
        name="gla_proj",
    )(x, g, w, wlo, wup, bup)


def _split2(g):
    hi = g.astype(BF16)
    lo = (g - hi.astype(F32)).astype(BF16)
    return hi, lo


def _gla_core_kernel(q_ref, k_ref, v_ref, gf_ref, gb_ref, r_ref, gn_ref, o_ref,
                     oacc_ref, sf_ref, sb_ref, *, L, T):
    C = GLA_CHUNK
    nc = T // C
    nsub = L // T
    DK = GLA_DK

    row = lax.broadcasted_iota(jnp.int32, (T, T), 0)
    col = lax.broadcasted_iota(jnp.int32, (T, T), 1)
    cbits = C.bit_length() - 1
    same = lax.shift_right_logical(row, cbits) == lax.shift_right_logical(col, cbits)
    mask_f = same & (col <= row)
    mask_b = same & (col > row)
    tri = jnp.where(mask_f, 1.0, 0.0).astype(BF16)

    def chunk_last(x):
        n = x.shape[1]
        x3 = x.reshape(nc, C, n)
        return jnp.broadcast_to(x3[:, C - 1:C, :], (nc, C, n)).reshape(T, n)

    def prefix(g):
        hi, lo = _split2(g)
        return _dot(tri, hi) + _dot(tri, lo)

    def decay_cols(tot_row):
        return jnp.transpose(jnp.broadcast_to(jnp.exp(tot_row), (DK, DK)))

    def apply_decay(s, dec):
        return jnp.concatenate([s[:, :DK] * dec, s[:, DK:] * dec], axis=1)

    def block_rows(t):
        return pl.ds(pl.multiple_of(t * T, T), T)

    def direction(rows, lg, tot, mask, s_ref, order):
        q = q_ref[0, rows, :]
        k = k_ref[0, rows, :]
        v = v_ref[0, rows, :]
        qd = (q * jnp.exp(lg)).astype(BF16)
        ki = (k * jnp.exp(-lg)).astype(BF16)
        ke = (k * jnp.exp(tot - lg)).astype(BF16)
        a = jnp.where(mask, _dot_nt(qd, ki), 0.0).astype(BF16)
        o = _dot(a, v)
        outs = [None] * nc
        for c in order:
            sl = slice(c * C, (c + 1) * C)
            s = s_ref[...]
            outs[c] = o[sl] + _dot(qd[sl], s.astype(BF16))
            dec = decay_cols(tot[c * C:c * C + 1, :])
            s_ref[...] = apply_decay(s, dec) + _dot_tn(ke[sl], v[sl])
        return jnp.concatenate(outs, axis=0)

    gn = gn_ref[...]

    def finish(rows, o):
        r = r_ref[0, rows, :]
        o_ref[0, rows, :] = (_rms(o, gn) * (r * jax.nn.sigmoid(r))).astype(BF16)

    def pair(i, first):
        rows_f = block_rows(i)
        rows_b = block_rows(nsub - 1 - i)
        gb = gb_ref[0, rows_b, :]
        p = prefix(jnp.concatenate([gf_ref[0, rows_f, :], gb], axis=1))
        tot = chunk_last(p)
        bf, totf = p[:, :DK], tot[:, :DK]
        pb, totb = p[:, DK:], tot[:, DK:]
        rb = totb - pb + gb
        o_f = direction(rows_f, bf, totf, mask_f, sf_ref, range(nc))
        o_b = direction(rows_b, rb, totb, mask_b, sb_ref, reversed(range(nc)))
        if first:
            oacc_ref[rows_f, :] = o_f
            oacc_ref[rows_b, :] = o_b
        else:
            finish(rows_f, oacc_ref[rows_f, :] + o_f)
            finish(rows_b, oacc_ref[rows_b, :] + o_b)

    sf_ref[...] = jnp.zeros_like(sf_ref)
    sb_ref[...] = jnp.zeros_like(sb_ref)
    half = nsub // 2
    unroll = next(u for u in (8, 4, 2, 1) if half % u == 0)
    lax.fori_loop(0, half, lambda i, c: (pair(i, True), c)[1], 0, unroll=unroll)
    lax.fori_loop(half, nsub, lambda i, c: (pair(i, False), c)[1], 0, unroll=unroll)


def _gla_core(q, k, v, gf, gb, r, gn):
    B, L, _ = q.shape
    T = min(GLA_SUB, L)
    assert L % (2 * T) == 0 and T % GLA_CHUNK == 0
    qk = pl.BlockSpec((1, L, GLA_DK), lambda b, h: (b, 0, h))
    vv = pl.BlockSpec((1, L, GLA_DV), lambda b, h: (b, 0, h))
    return pl.pallas_call(
        functools.partial(_gla_core_kernel, L=L, T=T),
        grid=(B, GLA_HEADS),
        in_specs=[qk, qk, vv, qk, qk, vv, _const_spec((1, GLA_DV))],
        out_specs=vv,
        out_shape=jax.ShapeDtypeStruct((B, L, GLA_V_W), BF16),
        scratch_shapes=[pltpu.VMEM((L, GLA_DV), F32),
                        pltpu.VMEM((GLA_DK, GLA_DV), F32),
                        pltpu.VMEM((GLA_DK, GLA_DV), F32)],
        compiler_params=_params("arbitrary", "arbitrary"),
        name="gla_core",
    )(q, k, v, gf, gb, r, gn)


def _mla_proj_kernel(x_ref, g_ref, win_ref, qn_ref, kvn_ref, wuq_ref, wukv_ref, cos_ref, sin_ref,
                     qt_ref, k_ref, vt_ref):
    H = MLA_HEADS
    xn = _rms(x_ref[...], g_ref[...]).astype(BF16)
    h = _dot(xn, win_ref[...])
    cq = h[:, :MLA_Q_RANK]
    ckv = h[:, MLA_Q_RANK:MLA_Q_RANK + MLA_KV_RANK]
    o = MLA_Q_RANK + MLA_KV_RANK
    cos = cos_ref[...]
    sin = sin_ref[...]
    kr = (h[:, o:o + LANES] * cos + h[:, o + LANES:o + 2 * LANES] * sin).astype(BF16)
    qa = _dot(_rms(cq, qn_ref[...]).astype(BF16), wuq_ref[...])
    kv = _dot(_rms(ckv, kvn_ref[...]).astype(BF16), wukv_ref[...])
    scale = (MLA_NOPE + MLA_ROPE) ** -0.5 * LOG2_E
    lane = lax.broadcasted_iota(jnp.int32, (1, LANES), 1)
    ro = H * MLA_NOPE
    rw = H * MLA_ROPE
    for hd in range(H):
        nope = qa[:, hd * LANES:(hd + 1) * LANES]
        g = hd // 2
        rp = qa[:, ro + g * LANES:ro + (g + 1) * LANES]
        rs = qa[:, ro + rw + g * LANES:ro + rw + (g + 1) * LANES]
        own = (lane < MLA_ROPE) if hd % 2 == 0 else (lane >= MLA_ROPE)
        rope = jnp.where(own, (rp * cos + rs * sin) * scale, 0.0)
        qt_ref[0, hd * MLA_QK_PAD:hd * MLA_QK_PAD + LANES, :] = jnp.transpose(nope * scale).astype(BF16)
        qt_ref[0, hd * MLA_QK_PAD + LANES:(hd + 1) * MLA_QK_PAD, :] = jnp.transpose(rope).astype(BF16)
        k_ref[:, hd * MLA_QK_PAD:hd * MLA_QK_PAD + LANES] = kv[:, hd * LANES:(hd + 1) * LANES].astype(BF16)
        k_ref[:, hd * MLA_QK_PAD + LANES:(hd + 1) * MLA_QK_PAD] = kr
    vt_ref[0] = jnp.transpose(kv[:, H * LANES:]).astype(BF16)


def _mla_proj(x, g, win, qn, kvn, wuq, wukv, cos, sin, *, S):
    T, D = x.shape
    tm = _row_tile(S, TOKEN_TILE)
    ns = S // tm
    row = lambda n: pl.BlockSpec((tm, n), lambda i: (i, 0))
    pos = pl.BlockSpec((tm, LANES), lambda i: (i % ns, 0))
    QW = MLA_HEADS * MLA_QK_PAD
    VW = MLA_HEADS * MLA_V
    return pl.pallas_call(
        _mla_proj_kernel,
        grid=(T // tm,),
        in_specs=[row(D), _const_spec((1, D)), _const_spec(win.shape), _const_spec(qn.shape),
                  _const_spec(kvn.shape), _const_spec(wuq.shape), _const_spec(wukv.shape), pos, pos],
        out_specs=[pl.BlockSpec((1, QW, tm), lambda i: (i // ns, 0, i % ns)), row(QW),
                   pl.BlockSpec((1, VW, tm), lambda i: (i // ns, 0, i % ns))],
        out_shape=[jax.ShapeDtypeStruct((T // S, QW, S), BF16),
                   jax.ShapeDtypeStruct((T, QW), BF16),
                   jax.ShapeDtypeStruct((T // S, VW, S), BF16)],
        compiler_params=_params("arbitrary"),
        name="mla_proj",
    )(x, g, win, qn, kvn, wuq, wukv, cos, sin)


def _mla_attn_kernel(qt_ref, k_ref, vt_ref, o_ref, s0_ref, s1_ref, m0_ref, m1_ref, *, nt):
    i = pl.program_id(0)
    s_refs = (s0_ref, s1_ref)
    m_refs = (m0_ref, m1_ref)

    S = k_ref.shape[1]
    kc = min(MLA_KEY_CHUNK, S)

    def step(score_slot, finish_slot):
        n = S // kc
        state = {}

        def score_chunk(j):
            rows = slice(j * kc, (j + 1) * kc)
            st = _dot(k_ref[0, rows, :], qt_ref[0])
            s_refs[score_slot][rows, :] = st
            mj = jnp.max(st, axis=0, keepdims=True)
            state["m"] = mj if j == 0 else jnp.maximum(state["m"], mj)
            if j == n - 1:
                m_refs[score_slot][...] = state["m"]

        def finish_chunk(j):
            rows = slice(j * kc, (j + 1) * kc)
            p = jnp.exp2(s_refs[finish_slot][rows, :] - m_refs[finish_slot][...])
            lj = jnp.sum(p, axis=0, keepdims=True)
            aj = _dot(vt_ref[0, :, rows], p.astype(BF16))
            state["l"] = lj if j == 0 else state["l"] + lj
            state["acc"] = aj if j == 0 else state["acc"] + aj
            if j == n - 1:
                o_ref[0] = jnp.transpose(state["acc"] / state["l"]).astype(BF16)

        if finish_slot is None:
            order = [("s", j) for j in range(n)]
        elif score_slot is None:
            order = [("f", j) for j in range(n)]
        else:
            order = [("s", 0), ("f", 0)]
            for j in range(1, n):
                order += [("f", j), ("s", j)]
        for kind, j in order:
            (score_chunk if kind == "s" else finish_chunk)(j)

    @pl.when(i == 0)
    def _():
        step(0, None)

    for par in range(2):
        @pl.when((i > 0) & (i < nt) & (i % 2 == par))
        def _():
            step(par, 1 - par)

    @pl.when(i == nt)
    def _():
        step(None, (nt - 1) % 2)


def _mla_attn(qt, k, vt):
    B, S, _ = k.shape
    tq = _row_tile(S, MLA_SCORE_BYTES // (2 * 4 * S))
    nq = S // tq
    nt = B * MLA_HEADS * nq

    def tile(t):
        bh = t // nq
        return bh // MLA_HEADS, bh % MLA_HEADS, t % nq

    def scored(t):
        return tile(jnp.minimum(t, nt - 1))

    def finished(t):
        return tile(jnp.maximum(t - 1, 0))

    def q_map(t):
        b, h, i = scored(t)
        return b, h, i

    def k_map(t):
        b, h, _ = scored(t)
        return b, 0, h

    def v_map(t):
        b, h, _ = finished(t)
        return b, h, 0

    def o_map(t):
        b, h, i = finished(t)
        return b, i, h

    return pl.pallas_call(
        functools.partial(_mla_attn_kernel, nt=nt),
        grid=(nt + 1,),
        in_specs=[
            pl.BlockSpec((1, MLA_QK_PAD, tq), q_map),
            pl.BlockSpec((1, S, MLA_QK_PAD), k_map),
            pl.BlockSpec((1, MLA_V, S), v_map),
        ],
        out_specs=pl.BlockSpec((1, tq, MLA_V), o_map),
        out_shape=jax.ShapeDtypeStruct((B, S, MLA_HEADS * MLA_V), BF16),
        scratch_shapes=[pltpu.VMEM((S, tq), F32), pltpu.VMEM((S, tq), F32),
                        pltpu.VMEM((1, tq), F32), pltpu.VMEM((1, tq), F32)],
        compiler_params=_params("arbitrary"),
        name="mla_attn",
    )(qt, k, vt)


def _prep_weights(ffn_norm, ffn_w_in, ffn_w_out, mix_norm, ple_norm, ple_w_gate, ple_w_proj,
                  gla_w_in, gla_w_gf_up, gla_b_gf, gla_w_gb_up, gla_b_gb, gla_out_norm, gla_w_out,
                  mla_w_in, mla_q_norm, mla_kv_norm, mla_w_uq, mla_w_ukv, mla_w_out, final_norm):
    nck = D_FF // FFN_CHUNK
    w = {}
    w["ffn_in"] = ffn_w_in.astype(BF16)
    w["ffn_out"] = ffn_w_out.reshape(DEPTH, 2, nck, FFN_CHUNK, D_MODEL).astype(BF16)
    w["ffn_norm"] = ffn_norm.reshape(DEPTH, 2, 1, D_MODEL)
    w["mix_norm"] = mix_norm.reshape(DEPTH, 1, D_MODEL)
    w["ple_norm"] = ple_norm.reshape(DEPTH, 1, D_MODEL)
    w["ple_gate"] = ple_w_gate.astype(BF16)
    w["ple_proj"] = ple_w_proj.astype(BF16)
    w["final_norm"] = final_norm.reshape(1, D_MODEL)

    main = 2 * GLA_QK_W + 2 * GLA_V_W
    NG = gla_w_in.shape[0]
    R = GLA_GATE_RANK
    w["gla_in"] = gla_w_in[..., :main].astype(BF16)
    w["gla_lo"] = jnp.pad(gla_w_in[..., main:], ((0, 0), (0, 0), (0, LANES - 2 * R))).astype(BF16)
    upm = jnp.zeros((NG, LANES, 2 * GLA_QK_W), F32)
    upm = upm.at[:, :R, :GLA_QK_W].set(gla_w_gf_up).at[:, R:2 * R, GLA_QK_W:].set(gla_w_gb_up)
    w["gla_up"] = upm.astype(BF16)
    w["gla_bup"] = jnp.concatenate([gla_b_gf, gla_b_gb], axis=-1).reshape(NG, 1, 2 * GLA_QK_W)
    w["gla_out_norm"] = gla_out_norm.reshape(NG, 1, GLA_DV)
    w["gla_out"] = gla_w_out.astype(BF16)

    NM = mla_w_in.shape[0]
    H = MLA_HEADS
    half = MLA_ROPE // 2
    swap = lambda t: jnp.concatenate([t[..., half:], t[..., :half]], axis=-1)
    twice = lambda t: jnp.concatenate([t, t], axis=-1)
    o = MLA_Q_RANK + MLA_KV_RANK
    kr = mla_w_in[..., o:]
    w["mla_in"] = jnp.concatenate([mla_w_in[..., :o], twice(kr), twice(swap(kr))], axis=-1).astype(BF16)
    uq = mla_w_uq.reshape(NM, MLA_Q_RANK, H, MLA_NOPE + MLA_ROPE)
    nope = uq[..., :MLA_NOPE].reshape(NM, MLA_Q_RANK, H * MLA_NOPE)
    rp = uq[..., MLA_NOPE:]
    w["mla_uq"] = jnp.concatenate(
        [nope, rp.reshape(NM, MLA_Q_RANK, H * MLA_ROPE), swap(rp).reshape(NM, MLA_Q_RANK, H * MLA_ROPE)],
        axis=-1).astype(BF16)
    ukv = mla_w_ukv.reshape(NM, MLA_KV_RANK, H, MLA_NOPE + MLA_V)
    w["mla_ukv"] = jnp.concatenate(
        [ukv[..., :MLA_NOPE].reshape(NM, MLA_KV_RANK, H * MLA_NOPE),
         ukv[..., MLA_NOPE:].reshape(NM, MLA_KV_RANK, H * MLA_V)], axis=-1).astype(BF16)
    w["mla_q_norm"] = mla_q_norm.reshape(NM, 1, MLA_Q_RANK)
    w["mla_kv_norm"] = mla_kv_norm.reshape(NM, 1, MLA_KV_RANK)
    w["mla_out"] = mla_w_out.astype(BF16)
    return w


def _rope_tables(S):
    inv_freq = ROPE_THETA ** (-jnp.arange(0, MLA_ROPE, 2, dtype=F32) / MLA_ROPE)
    ang = jnp.arange(S, dtype=F32)[:, None] * inv_freq[None, :]
    cos, sin = jnp.cos(ang), jnp.sin(ang)
    return jnp.concatenate([cos, cos, cos, cos], axis=1), jnp.concatenate([-sin, sin, -sin, sin], axis=1)


def _trunk(x, p, w):
    B, S, D = x.shape
    T = B * S
    x = x.reshape(T, D)
    p = p.reshape(DEPTH, T, PLE_DIM)
    cos, sin = _rope_tables(S)
    for i in range(DEPTH):
        j = i // 2
        x = _ffn(x, w, i, 0)
        if i % 2 == 0:
            q, k, v, r, gf, gb = _gla_proj(x, w["mix_norm"][i], w["gla_in"][j], w["gla_lo"][j],
                                           w["gla_up"][j], w["gla_bup"][j])
            sh = lambda t: t.reshape(B, S, t.shape[-1])
            o = _gla_core(sh(q), sh(k), sh(v), sh(gf), sh(gb), sh(r), w["gla_out_norm"][j])
            x = _post_mixer(x, o.reshape(T, GLA_V_W), w["gla_out"], p, w, i, j)
        else:
            qt, k, vt = _mla_proj(x, w["mix_norm"][i], w["mla_in"][j], w["mla_q_norm"][j], w["mla_kv_norm"][j],
                                  w["mla_uq"][j], w["mla_ukv"][j], cos, sin, S=S)
            o = _mla_attn(qt, k.reshape(B, S, k.shape[-1]), vt)
            x = _post_mixer(x, o.reshape(T, MLA_HEADS * MLA_V), w["mla_out"], p, w, i, j)
    return x.reshape(B, S, D)


def kernel(x_prompt, x_sample, p_prompt, p_sample, ffn_norm, ffn_w_in, ffn_w_out, mix_norm, ple_norm,
           ple_w_gate, ple_w_proj, gla_w_in, gla_w_gf_up, gla_b_gf, gla_w_gb_up, gla_b_gb, gla_out_norm,
           gla_w_out, mla_w_in, mla_q_norm, mla_kv_norm, mla_w_uq, mla_w_ukv, mla_w_out, final_norm):
    w = _prep_weights(ffn_norm, ffn_w_in, ffn_w_out, mix_norm, ple_norm, ple_w_gate, ple_w_proj,
                      gla_w_in, gla_w_gf_up, gla_b_gf, gla_w_gb_up, gla_b_gb, gla_out_norm, gla_w_out,
                      mla_w_in, mla_q_norm, mla_kv_norm, mla_w_uq, mla_w_ukv, mla_w_out, final_norm)
    return (_trunk(x_prompt, p_prompt, w), _trunk(x_sample, p_sample, w))
```
